```python
import math
import jax, jax.numpy as jnp
from jax import lax
import numpy as np

D_MODEL = 2048
BATCH = 1
SEQ = 8192
DEPTH = 2

GRID_W = 64
CTX_LEN = 256
GLA_HEADS = D_MODEL // 512
GLA_DK = 128
GLA_DV = 128
GLA_GATE_RANK = 16
GLA_GATE_NORM = 16.0
GLA_CHUNK = 64
DIFF_HEADS = D_MODEL // 512
DIFF_DQK = 64
DIFF_DV = 128
Q_BLOCK = 128
DELTA_HEADS = D_MODEL // 256
DELTA_DK = 128
DELTA_DV = 128
DELTA_CONV = 5
DELTA_CHUNK = 64
ROPE_BASE = 10000.0
RMS_EPS = 1e-6
L2_EPS = 1e-6
D_FF = ((8 * D_MODEL + 3 * 256 - 1) // (3 * 256)) * 256
GLA_W = GLA_HEADS * GLA_DV
DIFF_W = DIFF_HEADS * DIFF_DV
DELTA_W = DELTA_HEADS * DELTA_DV
IN_SIZES = (
    GLA_HEADS * GLA_DK, GLA_HEADS * GLA_DK, GLA_W, 2 * GLA_GATE_RANK, GLA_W,
    DIFF_HEADS * 2 * DIFF_DQK, DIFF_HEADS * 2 * DIFF_DQK, DIFF_W,
    DELTA_HEADS * DELTA_DK, DELTA_HEADS * DELTA_DK, DELTA_W,
    2 * DELTA_HEADS, 2 * DELTA_HEADS, DELTA_W,
    3 * D_MODEL,
)
N_IN = sum(IN_SIZES)

kernel_name = 'hybrid_gla_diffattn_gdn_block'


def rmsnorm(x, w):
    xf = x.astype(jnp.float32)
    y = xf * lax.rsqrt(jnp.mean(xf * xf, axis=-1, keepdims=True) + RMS_EPS)
    return (y * w.astype(jnp.float32)).astype(x.dtype)


def l2norm(x):
    xf = x.astype(jnp.float32)
    return (xf * lax.rsqrt(jnp.sum(xf * xf, axis=-1, keepdims=True) + L2_EPS)).astype(x.dtype)


def modulate(h, shift, scale):
    return h * (1.0 + scale) + shift


def to_heads(t, n_heads):
    b, l, _ = t.shape
    return t.reshape(b, l, n_heads, -1).transpose(0, 2, 1, 3)


def from_heads(t):
    b, h, l, d = t.shape
    return t.transpose(0, 2, 1, 3).reshape(b, l, h * d)


def split_in(z):
    return jnp.split(z, np.cumsum(IN_SIZES)[:-1].tolist(), axis=-1)


def flip_seq(t):
    return jnp.flip(t, axis=2)


def axial_rope_tables(n_tokens):
    rows = n_tokens // GRID_W
    row_ids = jnp.repeat(jnp.arange(rows, dtype=jnp.float32), GRID_W)
    col_ids = jnp.tile(jnp.arange(GRID_W, dtype=jnp.float32), rows)
    half = DIFF_DQK // 2
    inv = 1.0 / (ROPE_BASE ** (jnp.arange(0, half, 2, dtype=jnp.float32) / half))
    ang_r = row_ids[:, None] * inv
    ang_c = col_ids[:, None] * inv
    return (jnp.cos(ang_r), jnp.sin(ang_r), jnp.cos(ang_c), jnp.sin(ang_c))


def rotate(x, cos, sin):
    x1, x2 = jnp.split(x, 2, axis=-1)
    return jnp.concatenate([x1 * cos - x2 * sin, x2 * cos + x1 * sin], axis=-1)


def apply_axial_rope(x, rope):
    cr, sr, cc, sc = rope
    xr, xc = jnp.split(x, 2, axis=-1)
    return jnp.concatenate([rotate(xr, cr, sr), rotate(xc, cc, sc)], axis=-1).astype(x.dtype)


def centred_depthwise_conv(x, w):
    k_size = w.shape[0]
    pad = k_size // 2
    n = x.shape[1]
    xp = jnp.pad(x, ((0, 0), (pad, pad), (0, 0)))
    out = xp[:, 0:n] * w[0]
    for i in range(1, k_size):
        out = out + xp[:, i:i + n] * w[i]
    return out


def bidirectional_scan(scan_fn, shared_ctx, gates_ctx, shared_lat, gates_lat, s0):
    o_cf, s_f = scan_fn(*shared_ctx, *gates_ctx[0], s0)
    o_cb, s_b = scan_fn(*[flip_seq(t) for t in shared_ctx], *[flip_seq(t) for t in gates_ctx[1]], s0)
    o_lf, _ = scan_fn(*shared_lat, *gates_lat[0], s_f)
    o_lb, _ = scan_fn(*[flip_seq(t) for t in shared_lat], *[flip_seq(t) for t in gates_lat[1]], s_b)
    return o_lf + flip_seq(o_lb), o_cf + flip_seq(o_cb)


def gla_chunk_scan(q, k, v, g, s0):
    b, h, l, _ = q.shape
    dv = v.shape[-1]
    n = l // GLA_CHUNK

    def chunks(t):
        return jnp.moveaxis(t.astype(jnp.float32).reshape(b, h, n, GLA_CHUNK, t.shape[-1]), 2, 0)

    qc, kc, vc = chunks(q), chunks(k), chunks(v)
    bc = lax.cumsum(chunks(g), axis=3)
    causal = jnp.tril(jnp.ones((GLA_CHUNK, GLA_CHUNK), dtype=bool))[:, :, None]

    def step(state, xs):
        qi, ki, vi, bi = xs
        decay = jnp.exp(jnp.where(causal, bi[:, :, :, None, :] - bi[:, :, None, :, :], -jnp.inf))
        att = jnp.einsum('bhid,bhjd,bhijd->bhij', qi, ki, decay)
        o = jnp.einsum('bhij,bhjv->bhiv', att, vi) + jnp.einsum('bhid,bhdv->bhiv', qi * jnp.exp(bi), state)
        b_last = bi[:, :, -1:, :]
        state = state * jnp.exp(b_last)[:, :, 0, :, None] + jnp.einsum('bhjd,bhjv->bhdv', ki * jnp.exp(b_last - bi), vi)
        return state, o

    s_final, o = lax.scan(step, s0, (qc, kc, vc, bc))
    o = jnp.moveaxis(o, 0, 2).reshape(b, h, l, dv)
    return o.astype(v.dtype), s_final


def gated_delta_chunk_scan(q, k, v, g, beta, s0):
    f32 = jnp.float32
    b, h, l, dk = q.shape
    dv = v.shape[-1]
    cs = DELTA_CHUNK
    n = l // cs
    qc = q.astype(f32).reshape(b, h, n, cs, dk)
    kc = k.astype(f32).reshape(b, h, n, cs, dk)
    vc = v.astype(f32).reshape(b, h, n, cs, dv)
    gc = lax.cumsum(g.astype(f32).reshape(b, h, n, cs), axis=3)
    bc = beta.astype(f32).reshape(b, h, n, cs)[..., None]
    incl = jnp.tril(jnp.ones((cs, cs), dtype=bool))
    strict = jnp.tril(jnp.ones((cs, cs), dtype=bool), -1)
    decay = jnp.exp(jnp.where(incl, gc[..., :, None] - gc[..., None, :], -jnp.inf))
    kb = kc * bc
    a_strict = jnp.where(strict, jnp.einsum('bhnid,bhnjd->bhnij', kb, kc) * decay, 0.0)
    rhs = jnp.concatenate([vc * bc, kb * jnp.exp(gc)[..., None]], axis=-1)
    sol = lax.linalg.triangular_solve(jnp.eye(cs, dtype=f32) + a_strict, rhs, left_side=True, lower=True)
    u, w = sol[..., :dv], sol[..., dv:]
    att = jnp.einsum('bhnid,bhnjd->bhnij', qc, kc) * decay
    q_dec = qc * jnp.exp(gc)[..., None]
    k_dec = kc * jnp.exp(gc[..., -1:] - gc)[..., None]
    g_last = jnp.exp(gc[..., -1])[..., None, None]
    xs = tuple(jnp.moveaxis(t, 2, 0) for t in (u, w, att, q_dec, k_dec, g_last))

    def step(state, xs_i):
        u_i, w_i, a_i, qd_i, kd_i, gl_i = xs_i
        v_new = u_i - jnp.einsum('bhcd,bhdv->bhcv', w_i, state)
        o = jnp.einsum('bhcd,bhdv->bhcv', qd_i, state) + jnp.einsum('bhij,bhjv->bhiv', a_i, v_new)
        state = state * gl_i + jnp.einsum('bhjd,bhjv->bhdv', kd_i, v_new)
        return state, o

    s_final, o = lax.scan(step, s0, xs)
    o = jnp.moveaxis(o, 0, 2).reshape(b, h, l, dv)
    return o.astype(v.dtype), s_final


def gla_prep(q, k, v, glr, w2, bias):
    qh = to_heads(q, GLA_HEADS) * (GLA_DK ** -0.5)
    kh = to_heads(k, GLA_HEADS)
    vh = to_heads(v, GLA_HEADS)
    gates = []
    for i, lr in enumerate(jnp.split(glr, 2, axis=-1)):
        gk = jax.nn.log_sigmoid((lr @ w2[i] + bias[i]).astype(jnp.float32)) / GLA_GATE_NORM
        gates.append((to_heads(gk, GLA_HEADS),))
    return (qh, kh, vh), (gates[0], gates[1])


def delta_prep(q, k, v, a, bt, conv_w, a_log, dt_bias):
    qkv = jax.nn.silu(centred_depthwise_conv(jnp.concatenate([q, k, v], axis=-1), conv_w))
    q, k, v = jnp.split(qkv, 3, axis=-1)
    qh = l2norm(to_heads(q, DELTA_HEADS)) * (DELTA_DK ** -0.5)
    kh = l2norm(to_heads(k, DELTA_HEADS))
    vh = to_heads(v, DELTA_HEADS)
    a_dirs = jnp.split(a.astype(jnp.float32), 2, axis=-1)
    b_dirs = jnp.split(bt.astype(jnp.float32), 2, axis=-1)
    gates = []
    for i in range(2):
        g = -jnp.exp(a_log[i].astype(jnp.float32)) * jax.nn.softplus(a_dirs[i] + dt_bias[i].astype(jnp.float32))
        beta = jax.nn.sigmoid(b_dirs[i])
        gates.append((jnp.swapaxes(g, 1, 2), jnp.swapaxes(beta, 1, 2)))
    return (qh, kh, vh), (gates[0], gates[1])


def diff_heads(t):
    b, l, _ = t.shape
    return t.reshape(b, l, DIFF_HEADS, 2, DIFF_DQK).transpose(0, 2, 3, 1, 4)


def diff_attend(q, k, v, lam):
    s = jnp.einsum('bhmqd,bhmkd->bhmqk', q, k).astype(jnp.float32) * (DIFF_DQK ** -0.5)
    p = jax.nn.softmax(s, axis=-1)
    a = p[:, :, 0] - lam * p[:, :, 1]
    return jnp.einsum('bhqk,bhkd->bhqd', a.astype(v.dtype), v)


def hybrid_mixer(h_lat, h_ctx, rope, lam_init, with_ctx_out, w_in, gla_gate_w2, gla_gate_b, gla_norm_w,
                 diff_lambda, diff_norm_w, delta_conv_w, delta_a_log, delta_dt_bias, delta_norm_w,
                 w_up_gla, w_up_diff, w_up_delta, w_o):
    lat = split_in(h_lat @ w_in)
    ctx = split_in(h_ctx @ w_in)
    bsz = h_lat.shape[0]

    ga_lat, gg_lat = gla_prep(lat[0], lat[1], lat[2], lat[3], gla_gate_w2, gla_gate_b)
    ga_ctx, gg_ctx = gla_prep(ctx[0], ctx[1], ctx[2], ctx[3], gla_gate_w2, gla_gate_b)
    s0a = jnp.zeros((bsz, GLA_HEADS, GLA_DK, GLA_DV), jnp.float32)
    oa_lat, oa_ctx = bidirectional_scan(gla_chunk_scan, ga_ctx, gg_ctx, ga_lat, gg_lat, s0a)

    lam_p = diff_lambda.astype(jnp.float32)
    lam = jnp.exp(jnp.sum(lam_p[0] * lam_p[1])) - jnp.exp(jnp.sum(lam_p[2] * lam_p[3])) + lam_init
    q_lat = apply_axial_rope(diff_heads(lat[5]), rope)
    k_lat = apply_axial_rope(diff_heads(lat[6]), rope)
    v_lat = to_heads(lat[7], DIFF_HEADS)
    k_ctx = diff_heads(ctx[6])
    v_ctx = to_heads(ctx[7], DIFF_HEADS)
    k_all = jnp.concatenate([k_ctx, k_lat], axis=3)
    v_all = jnp.concatenate([v_ctx, v_lat], axis=2)
    _, nh, _, l, dq = q_lat.shape
    n_blk = l // Q_BLOCK
    q_blocks = jnp.moveaxis(q_lat.reshape(bsz, nh, 2, n_blk, Q_BLOCK, dq), 3, 0)
    ob = lax.map(lambda qb: diff_attend(qb, k_all, v_all, lam), q_blocks)
    od_lat = jnp.moveaxis(ob, 0, 2).reshape(bsz, nh, l, DIFF_DV)

    ea_lat, eg_lat = delta_prep(lat[8], lat[9], lat[10], lat[11], lat[12], delta_conv_w, delta_a_log, delta_dt_bias)
    ea_ctx, eg_ctx = delta_prep(ctx[8], ctx[9], ctx[10], ctx[11], ctx[12], delta_conv_w, delta_a_log, delta_dt_bias)
    s0e = jnp.zeros((bsz, DELTA_HEADS, DELTA_DK, DELTA_DV), jnp.float32)
    oe_lat, oe_ctx = bidirectional_scan(gated_delta_chunk_scan, ea_ctx, eg_ctx, ea_lat, eg_lat, s0e)

    def finish(oa, od, oe, p):
        a = from_heads(rmsnorm(oa, gla_norm_w)) * jax.nn.silu(p[4])
        d = from_heads(rmsnorm(od, diff_norm_w) * (1.0 - lam_init))
        e = from_heads(rmsnorm(oe, delta_norm_w)) * jax.nn.silu(p[13])
        g_a, g_d, g_e = jnp.split(jax.nn.sigmoid(p[14]), 3, axis=-1)
        y = g_a * (a @ w_up_gla) + g_d * (d @ w_up_diff) + g_e * (e @ w_up_delta)
        return y @ w_o

    y_lat = finish(oa_lat, od_lat, oe_lat, lat)
    if not with_ctx_out:
        return y_lat, None
    od_ctx = diff_attend(diff_heads(ctx[5]), k_ctx, v_ctx, lam)
    y_ctx = finish(oa_ctx, od_ctx, oe_ctx, ctx)
    return y_lat, y_ctx


def swiglu(h, w1, w3, w2):
    return (jax.nn.silu(h @ w1) * (h @ w3)) @ w2


def setup_inputs(seed: int = 0) -> dict:
    key = jax.random.key(seed)
    ks = jax.random.split(key, 32)
    f32 = jnp.float32

    def nrm(k, shape, scale):
        return jax.random.normal(k, shape, f32) * scale

    def gain(k, shape):
        return 1.0 + 0.02 * jax.random.normal(k, shape, f32)

    dt = jnp.exp(jax.random.uniform(ks[17], (DEPTH, 2, DELTA_HEADS), f32, math.log(0.001), math.log(0.1)))
    return {
        'x': nrm(ks[0], (BATCH, SEQ, D_MODEL), 1.0),
        'c': nrm(ks[1], (BATCH, D_MODEL), 1.0),
        'ctx': nrm(ks[2], (BATCH, CTX_LEN, D_MODEL), 1.0),
        'c_ctx': nrm(ks[3], (D_MODEL,), 1.0),
        'ada_w': nrm(ks[4], (DEPTH, D_MODEL, 6 * D_MODEL), 0.5 * D_MODEL ** -0.5),
        'ada_b': nrm(ks[5], (DEPTH, 6 * D_MODEL), 0.02),
        'mix_pre_w': gain(ks[6], (DEPTH, D_MODEL)),
        'mix_post_w': gain(ks[7], (DEPTH, D_MODEL)),
        'ffn_pre_w': gain(ks[8], (DEPTH, D_MODEL)),
        'ffn_post_w': gain(ks[9], (DEPTH, D_MODEL)),
        'w_in': nrm(ks[10], (DEPTH, D_MODEL, N_IN), D_MODEL ** -0.5),
        'gla_gate_w2': nrm(ks[11], (DEPTH, 2, GLA_GATE_RANK, GLA_HEADS * GLA_DK), GLA_GATE_RANK ** -0.5),
        'gla_gate_b': nrm(ks[12], (DEPTH, 2, GLA_HEADS * GLA_DK), 0.02),
        'gla_norm_w': gain(ks[13], (DEPTH, GLA_DV)),
        'diff_lambda': nrm(ks[14], (DEPTH, 4, DIFF_DQK), 0.1),
        'diff_norm_w': gain(ks[15], (DEPTH, DIFF_DV)),
        'delta_conv_w': nrm(ks[16], (DEPTH, DELTA_CONV, 3 * DELTA_W), DELTA_CONV ** -0.5),
        'delta_a_log': jnp.log(jax.random.uniform(ks[18], (DEPTH, 2, DELTA_HEADS), f32, 1.0, 16.0)),
        'delta_dt_bias': dt + jnp.log(-jnp.expm1(-dt)),
        'delta_norm_w': gain(ks[19], (DEPTH, DELTA_DV)),
        'w_up_gla': nrm(ks[20], (DEPTH, GLA_W, D_MODEL), GLA_W ** -0.5),
        'w_up_diff': nrm(ks[21], (DEPTH, DIFF_W, D_MODEL), DIFF_W ** -0.5),
        'w_up_delta': nrm(ks[22], (DEPTH, DELTA_W, D_MODEL), DELTA_W ** -0.5),
        'w_o': nrm(ks[23], (DEPTH, D_MODEL, D_MODEL), D_MODEL ** -0.5),
        'ffn_w1': nrm(ks[24], (DEPTH, D_MODEL, D_FF), D_MODEL ** -0.5),
        'ffn_w3': nrm(ks[25], (DEPTH, D_MODEL, D_FF), D_MODEL ** -0.5),
        'ffn_w2': nrm(ks[26], (DEPTH, D_FF, D_MODEL), D_FF ** -0.5),
    }


def reference(x, c, ctx, c_ctx, ada_w, ada_b, mix_pre_w, mix_post_w, ffn_pre_w, ffn_post_w, w_in,
              gla_gate_w2, gla_gate_b, gla_norm_w, diff_lambda, diff_norm_w, delta_conv_w, delta_a_log,
              delta_dt_bias, delta_norm_w, w_up_gla, w_up_diff, w_up_delta, w_o, ffn_w1, ffn_w3, ffn_w2):
    rope = axial_rope_tables(x.shape[1])
    for layer in range(DEPTH):
        with_ctx_out = layer < DEPTH - 1
        lam_init = 0.8 - 0.6 * math.exp(-0.3 * layer)
        mod_lat = jnp.split((jax.nn.silu(c) @ ada_w[layer] + ada_b[layer])[:, None, :], 6, axis=-1)
        mod_ctx = jnp.split(jax.nn.silu(c_ctx) @ ada_w[layer] + ada_b[layer], 6, axis=-1)
        h_lat = modulate(rmsnorm(x, mix_pre_w[layer]), mod_lat[0], mod_lat[1])
        h_ctx = modulate(rmsnorm(ctx, mix_pre_w[layer]), mod_ctx[0], mod_ctx[1])
        y_lat, y_ctx = hybrid_mixer(h_lat, h_ctx, rope, lam_init, with_ctx_out, w_in[layer],
                                    gla_gate_w2[layer], gla_gate_b[layer], gla_norm_w[layer],
                                    diff_lambda[layer], diff_norm_w[layer], delta_conv_w[layer],
                                    delta_a_log[layer], delta_dt_bias[layer], delta_norm_w[layer],
                                    w_up_gla[layer], w_up_diff[layer], w_up_delta[layer], w_o[layer])
        x = x + mod_lat[2] * rmsnorm(y_lat, mix_post_w[layer])
        h_lat = modulate(rmsnorm(x, ffn_pre_w[layer]), mod_lat[3], mod_lat[4])
        x = x + mod_lat[5] * rmsnorm(swiglu(h_lat, ffn_w1[layer], ffn_w3[layer], ffn_w2[layer]), ffn_post_w[layer])
        if with_ctx_out:
            ctx = ctx + mod_ctx[2] * rmsnorm(y_ctx, mix_post_w[layer])
            h_ctx = modulate(rmsnorm(ctx, ffn_pre_w[layer]), mod_ctx[3], mod_ctx[4])
            ctx = ctx + mod_ctx[5] * rmsnorm(swiglu(h_ctx, ffn_w1[layer], ffn_w3[layer], ffn_w2[layer]), ffn_post_w[layer])
    return x
```

```python
import functools
import math

import numpy as np
import jax
import jax.numpy as jnp
from jax import lax
from jax.experimental import pallas as pl
from jax.experimental.pallas import tpu as pltpu

F32 = jnp.float32
BF16 = jnp.bfloat16

D_MODEL = 2048
DEPTH = 2
GRID_W = 64
CTX_LEN = 256
HEAD = 128
GLA_HEADS = 4
GLA_GATE_RANK = 16
GLA_GATE_NORM = 16.0
DIFF_HEADS = 4
DIFF_DQK = 64
DELTA_HEADS = 8
DELTA_CONV = 5
CHUNK = 64
ROPE_BASE = 10000.0
RMS_EPS = 1e-6
L2_EPS = 1e-6
D_FF = 5632
GLA_W = GLA_HEADS * HEAD
DIFF_W = DIFF_HEADS * HEAD
DELTA_W = DELTA_HEADS * HEAD

COL_GQ, COL_GK, COL_GV, COL_GG = 0, 512, 1024, 1536
COL_EG = 2048
COL_EQ, COL_EK, COL_EV = 3072, 4096, 5120
COL_DQ, COL_DK, COL_DV = 6144, 6656, 7168
COL_MG = 7680
COL_SM = 13824
SM_W = 256
NZ = COL_SM + SM_W
SM_A = 32
SM_B = 48

ROW_TILE = 768
SCAN_ROWS = 256
IN_TN = 1280
FF_TN = 512
UP_TN = 512
ATT_TQ = 256
COND_ROWS = 16
VMEM_LIMIT = 56 * 1024 * 1024


def _cparams(sem, vmem=VMEM_LIMIT):
    return pltpu.CompilerParams(dimension_semantics=sem, vmem_limit_bytes=vmem)


def _bf(x):
    return x.astype(BF16)


def _dot(a, b):
    return jnp.dot(a, b, preferred_element_type=F32)


def _dot_nt(a, b):
    return lax.dot_general(a, b, (((1,), (1,)), ((), ())), preferred_element_type=F32)


def _dot_tn(a, b):
    return lax.dot_general(a, b, (((0,), (0,)), ((), ())), preferred_element_type=F32)


def _split3(x):
    hi = _bf(x)
    r = x - hi.astype(F32)
    mid = _bf(r)
    lo = _bf(r - mid.astype(F32))
    return hi, mid, lo


def _dot_exact_lhs(m_bf, x):
    hi, mid, lo = _split3(x)
    return _dot(m_bf, hi) + _dot(m_bf, mid) + _dot(m_bf, lo)


def _dot_exact_rhs(x, m_bf):
    hi, mid, lo = _split3(x)
    return _dot(hi, m_bf) + _dot(mid, m_bf) + _dot(lo, m_bf)


def _dot3(a, b):
    ah = _bf(a)
    al = _bf(a - ah.astype(F32))
    bh = _bf(b)
    bl = _bf(b - bh.astype(F32))
    return _dot(ah, bh) + _dot(ah, bl) + _dot(al, bh)


def _sigmoid(x):
    return 1.0 / (1.0 + jnp.exp(-x))


def _silu(x):
    return x * _sigmoid(x)


def _softplus(x):
    return jnp.maximum(x, 0.0) + jnp.log1p(jnp.exp(-jnp.abs(x)))


def _log_sigmoid(x):
    return jnp.minimum(x, 0.0) - jnp.log1p(jnp.exp(-jnp.abs(x)))


def _rms(x, w):
    return x * lax.rsqrt(jnp.mean(x * x, axis=-1, keepdims=True) + RMS_EPS) * w


def _row_is_ctx(row0, n):
    rows = row0 + lax.broadcasted_iota(jnp.int32, (n, 1), 0)
    return rows < CTX_LEN


def _pick_mod(is_ctx, mod_ref):
    return jnp.where(is_ctx, mod_ref[1:2, :], mod_ref[0:1, :])


def _ada_kernel(s_ref, w_ref, b_ref, o_ref):
    o_ref[0] = _dot(s_ref[...], _bf(w_ref[0])) + b_ref[0]


def _ada(s_bf, ada_w, ada_b):
    depth, d, n6 = ada_w.shape
    tn = 1024
    return pl.pallas_call(
        _ada_kernel,
        out_shape=jax.ShapeDtypeStruct((depth, COND_ROWS, n6), F32),
        grid=(depth, n6 // tn),
        in_specs=[
            pl.BlockSpec((COND_ROWS, d), lambda l, j: (0, 0)),
            pl.BlockSpec((1, d, tn), lambda l, j: (l, 0, j)),
            pl.BlockSpec((1, 1, tn), lambda l, j: (l, 0, j)),
        ],
        out_specs=pl.BlockSpec((1, COND_ROWS, tn), lambda l, j: (l, 0, j)),
        compiler_params=_cparams(("arbitrary", "arbitrary")),
        name="ada",
    )(s_bf, ada_w, ada_b.reshape(depth, 1, n6))


def _inproj_kernel(x_ref, nw_ref, sh_ref, sc_ref, w_ref, o_ref, h_scr):
    @pl.when(pl.program_id(1) == 0)
    def _():
        tm = x_ref.shape[0]
        is_ctx = _row_is_ctx(pl.program_id(0) * tm, tm)
        h = _rms(x_ref[...], nw_ref[...])
        h = h * (1.0 + _pick_mod(is_ctx, sc_ref)) + _pick_mod(is_ctx, sh_ref)
        h_scr[...] = _bf(h)

    o_ref[...] = _dot(h_scr[...], w_ref[...])


def _inproj(xc, norm_w, mods, w_bf, tm):
    ntok, d = xc.shape
    nz = w_bf.shape[1]
    return pl.pallas_call(
        _inproj_kernel,
        out_shape=jax.ShapeDtypeStruct((ntok, nz), F32),
        grid=(ntok // tm, nz // IN_TN),
        in_specs=[
            pl.BlockSpec((tm, d), lambda i, j: (i, 0)),
            pl.BlockSpec((1, d), lambda i, j: (0, 0)),
            pl.BlockSpec((8, d), lambda i, j: (0, 0)),
            pl.BlockSpec((8, d), lambda i, j: (0, 1)),
            pl.BlockSpec((d, IN_TN), lambda i, j: (0, j)),
        ],
        out_specs=pl.BlockSpec((tm, IN_TN), lambda i, j: (i, j)),
        scratch_shapes=[pltpu.VMEM((tm, d), BF16)],
        compiler_params=_cparams(("arbitrary", "arbitrary")),
        name="inproj",
    )(xc, norm_w.reshape(1, d), mods, mods, w_bf)


def _gla_consts(reverse):
    c = CHUNK
    i = np.arange(c)[:, None]
    t = np.arange(c)[None, :]
    if not reverse:
        mats = [t <= i, t > i]
    else:
        mats = [t >= i, t < i]
    masks = [i == t]
    for m in (32, 16, 8, 4, 2, 1):
        p = (i // (2 * m)) * (2 * m) + m - 1
        second = ((i // m) % 2) == 1
        same = (i // (2 * m)) == (t // (2 * m))
        t_second = ((t // m) % 2) == 1
        if not reverse:
            w = np.where(second, (t > p) & (t <= i), (t > i) & (t <= p))
            pm = same & second & ~t_second
        else:
            w = np.where(second, (t >= p + 1) & (t <= i - 1), (t >= i) & (t <= p))
            pm = same & ~second & t_second
        mats.append(w)
        masks.append(pm)
    wall = np.concatenate([m_.astype(np.float32) for m_ in mats], axis=0)
    pmask = np.stack([m_.astype(np.float32) for m_ in masks], axis=0)
    return jnp.asarray(wall, BF16), jnp.asarray(pmask, F32)


def _gla_kernel(q_ref, k_ref, v_ref, sm_ref, w2_ref, b_ref, wall_ref, pm_ref, o_ref, st_scr, g_scr,
                *, reverse):
    @pl.when(pl.program_id(0) == 0)
    def _():
        st_scr[...] = jnp.zeros_like(st_scr)

    x = _dot(_bf(sm_ref[:, 0:HEAD]), w2_ref[...]) + b_ref[...]
    g_scr[...] = _log_sigmoid(x) * (1.0 / GLA_GATE_NORM)
    n_chunks = q_ref.shape[0] // CHUNK
    last = 0 if reverse else CHUNK - 1
    scale = HEAD ** -0.5

    def chunk_body(ci, carry):
        cc = (n_chunks - 1 - ci) if reverse else ci
        r0 = pl.multiple_of(cc * CHUNK, CHUNK)
        rows = pl.ds(r0, CHUNK)
        e_all = jnp.exp(_dot_exact_lhs(wall_ref[...], g_scr[rows, :]))
        for h in range(GLA_HEADS):
            cols = slice(h * HEAD, (h + 1) * HEAD)
            q = q_ref[rows, cols] * scale
            k = k_ref[rows, cols]
            vb = _bf(v_ref[rows, cols])
            e = e_all[:, cols]
            eb = e[0:CHUNK]
            est = e[CHUNK:2 * CHUNK]
            st = st_scr[h]
            o = _dot_nt(_bf(q * eb), _bf(st))
            att = jnp.where(pm_ref[0] > 0.0, _dot_nt(_bf(q), _bf(k)), 0.0)
            for lv in range(6):
                el = e[(2 + lv) * CHUNK:(3 + lv) * CHUNK]
                att = att + jnp.where(pm_ref[lv + 1] > 0.0, _dot_nt(_bf(q * el), _bf(k * el)), 0.0)
            o = o + _dot(_bf(att), vb)
            o_ref[rows, cols] = o
            st_scr[h] = st * eb[last:last + 1, :] + _dot_tn(vb, _bf(k * est))
        return carry

    lax.fori_loop(0, n_chunks, chunk_body, 0)


def _scan_block_index(i, nblk, reverse):
    if not reverse:
        return i
    nctx = CTX_LEN // SCAN_ROWS
    return jnp.where(i < nctx, nctx - 1 - i, nblk - 1 + nctx - i)


def _gla(z, w2pad, bias, reverse):
    ntok = z.shape[0]
    nblk = ntok // SCAN_ROWS
    wall, pmask = _gla_consts(reverse)
    rb = functools.partial(_scan_block_index, nblk=nblk, reverse=reverse)
    return pl.pallas_call(
        functools.partial(_gla_kernel, reverse=reverse),
        out_shape=jax.ShapeDtypeStruct((ntok, GLA_W), F32),
        grid=(nblk,),
        in_specs=[
            pl.BlockSpec((SCAN_ROWS, GLA_W), lambda i: (rb(i), COL_GQ // GLA_W)),
            pl.BlockSpec((SCAN_ROWS, GLA_W), lambda i: (rb(i), COL_GK // GLA_W)),
            pl.BlockSpec((SCAN_ROWS, GLA_W), lambda i: (rb(i), COL_GV // GLA_W)),
            pl.BlockSpec((SCAN_ROWS, SM_W), lambda i: (rb(i), COL_SM // SM_W)),
            pl.BlockSpec((HEAD, GLA_W), lambda i: (0, 0)),
            pl.BlockSpec((1, GLA_W), lambda i: (0, 0)),
            pl.BlockSpec(wall.shape, lambda i: (0, 0)),
            pl.BlockSpec(pmask.shape, lambda i: (0, 0, 0)),
        ],
        out_specs=pl.BlockSpec((SCAN_ROWS, GLA_W), lambda i: (rb(i), 0)),
        scratch_shapes=[pltpu.VMEM((GLA_HEADS, HEAD, HEAD), F32), pltpu.VMEM((SCAN_ROWS, GLA_W), F32)],
        compiler_params=_cparams(("arbitrary",)),
        name="gla_bwd" if reverse else "gla_fwd",
    )(z, z, z, z, w2pad, bias, wall, pmask)


def _rope_kernel(q_ref, k_ref, v_ref, cos_ref, sin_ref, qo_ref, ko_ref, vo_ref):
    w = q_ref.shape[1]
    cos = jnp.concatenate([cos_ref[...]] * DIFF_HEADS, axis=1)
    sin = jnp.concatenate([sin_ref[...]] * DIFF_HEADS, axis=1)
    lane = lax.broadcasted_iota(jnp.int32, q_ref.shape, 1)
    first = (lane % 32) < 16

    def rot(x):
        swapped = jnp.where(first, pltpu.roll(x, w - 16, axis=1), pltpu.roll(x, 16, axis=1))
        return x * cos + swapped * sin

    qo_ref[...] = _bf(rot(q_ref[...]) * (DIFF_DQK ** -0.5))
    ko_ref[...] = _bf(rot(k_ref[...]))
    vo_ref[...] = _bf(v_ref[...])


def _rope(z, cos_t, sin_t):
    ntok = z.shape[0]
    tm = SCAN_ROWS
    spec = lambda col: pl.BlockSpec((tm, DIFF_W), lambda i: (i, col // DIFF_W))
    out = jax.ShapeDtypeStruct((ntok, DIFF_W), BF16)
    return pl.pallas_call(
        _rope_kernel,
        out_shape=(out, out, out),
        grid=(ntok // tm,),
        in_specs=[spec(COL_DQ), spec(COL_DK), spec(COL_DV),
                  pl.BlockSpec((tm, HEAD), lambda i: (i, 0)),
                  pl.BlockSpec((tm, HEAD), lambda i: (i, 0))],
        out_specs=(pl.BlockSpec((tm, DIFF_W), lambda i: (i, 0)),) * 3,
        compiler_params=_cparams(("arbitrary",)),
        name="rope",
    )(z, z, z, cos_t, sin_t)


def _attn_kernel(lam_ref, q_ref, k_ref, v_ref, o_ref, *, n_kv, tk, lam_init):
    tq = q_ref.shape[0]
    q = q_ref[...]
    lane = lax.broadcasted_iota(jnp.int32, q.shape, 1)
    zero = jnp.zeros_like(q)
    q2 = jnp.concatenate([jnp.where(lane < DIFF_DQK, q, zero), jnp.where(lane >= DIFF_DQK, q, zero)], axis=0)

    def body(c, carry):
        m, l, acc = carry
        rows = pl.ds(pl.multiple_of(c * tk, tk), tk)
        s = _dot_nt(q2, k_ref[rows, :])
        m_new = jnp.maximum(m, jnp.max(s, axis=-1, keepdims=True))
        alpha = jnp.exp(m - m_new)
        p = jnp.exp(s - m_new)
        l = alpha * l + jnp.sum(p, axis=-1, keepdims=True)
        acc = alpha * acc + _dot(_bf(p), v_ref[rows, :])
        return m_new, l, acc

    init = (jnp.full((2 * tq, 1), -jnp.inf, F32), jnp.zeros((2 * tq, 1), F32), jnp.zeros((2 * tq, HEAD), F32))
    _, l, acc = lax.fori_loop(0, n_kv, body, init)
    o = acc / l
    lp = lam_ref[...]
    lam = (jnp.exp(jnp.sum(lp[0:1] * lp[1:2], axis=-1, keepdims=True))
           - jnp.exp(jnp.sum(lp[2:3] * lp[3:4], axis=-1, keepdims=True)) + lam_init)
    o_ref[...] = o[0:tq] - lam * o[tq:2 * tq]


def _attn(lam_p, q, k, v, q_row0, n_q, kv_rows, tk, lam_init):
    tq = ATT_TQ
    qb0 = q_row0 // tq
    return pl.pallas_call(
        functools.partial(_attn_kernel, n_kv=kv_rows // tk, tk=tk, lam_init=lam_init),
        out_shape=jax.ShapeDtypeStruct((n_q, DIFF_W), F32),
        grid=(DIFF_HEADS, n_q // tq),
        in_specs=[
            pl.BlockSpec(lam_p.shape, lambda h, i: (0, 0)),
            pl.BlockSpec((tq, HEAD), lambda h, i: (i + qb0, h)),
            pl.BlockSpec((kv_rows, HEAD), lambda h, i: (0, h)),
            pl.BlockSpec((kv_rows, HEAD), lambda h, i: (0, h)),
        ],
        out_specs=pl.BlockSpec((tq, HEAD), lambda h, i: (i, h)),
        compiler_params=_cparams(("arbitrary", "arbitrary")),
        name="diff_attn",
    )(lam_p, q, k, v)


def _conv_kernel(xp_ref, x_ref, xn_ref, w_ref, o_ref):
    i = pl.program_id(0)
    nblk = pl.num_programs(0)
    nctx = CTX_LEN // SCAN_ROWS
    tm = x_ref.shape[0]
    pad = DELTA_CONV // 2
    has_prev = jnp.logical_and(i != 0, i != nctx)
    has_next = jnp.logical_and(i != nctx - 1, i != nblk - 1)
    prev = jnp.where(has_prev, xp_ref[...], 0.0)
    nxt = jnp.where(has_next, xn_ref[...], 0.0)
    xe = jnp.concatenate([prev, x_ref[...], nxt], axis=0)
    acc = None
    for j in range(DELTA_CONV):
        off = 8 + j - pad
        term = xe[off:off + tm, :] * w_ref[j:j + 1, :]
        acc = term if acc is None else acc + term
    y = _silu(acc)
    jc = pl.program_id(1)
    q_blocks = DELTA_W // x_ref.shape[1]
    is_qk = jc < 2 * q_blocks
    post = jnp.where(jc < q_blocks, HEAD ** -0.5, 1.0)
    for hh in range(x_ref.shape[1] // HEAD):
        cols = slice(hh * HEAD, (hh + 1) * HEAD)
        yh = y[:, cols]
        yn = yh * lax.rsqrt(jnp.sum(yh * yh, axis=-1, keepdims=True) + L2_EPS) * post
        o_ref[:, cols] = jnp.where(is_qk, yn, yh)


def _conv(z, conv_w):
    ntok = z.shape[0]
    tm = SCAN_ROWS
    width = 3 * DELTA_W
    nb8 = ntok // 8
    r8 = tm // 8
    cw = 512
    ncb = width // cw
    c0 = COL_EQ // cw
    return pl.pallas_call(
        _conv_kernel,
        out_shape=jax.ShapeDtypeStruct((ntok, width), F32),
        grid=(ntok // tm, ncb),
        in_specs=[
            pl.BlockSpec((8, cw), lambda i, j: (jnp.maximum(i * r8 - 1, 0), c0 + j)),
            pl.BlockSpec((tm, cw), lambda i, j: (i, c0 + j)),
            pl.BlockSpec((8, cw), lambda i, j: (jnp.minimum((i + 1) * r8, nb8 - 1), c0 + j)),
            pl.BlockSpec((DELTA_CONV, cw), lambda i, j: (0, j)),
        ],
        out_specs=pl.BlockSpec((tm, cw), lambda i, j: (i, j)),
        compiler_params=_cparams(("arbitrary", "arbitrary")),
        name="delta_conv",
    )(z, z, z, conv_w)


def _delta_consts(reverse):
    c = CHUNK
    i = np.arange(c)[:, None]
    t = np.arange(c)[None, :]
    tri = (t >= i) if reverse else (t <= i)
    incl = tri
    strict = (t > i) if reverse else (t < i)
    cc = np.arange(4 * c)[None, :]
    j = cc % c
    eye = (i == j)
    m16 = (i // 16) == (j // 16)
    q32 = ((i // 32) == (j // 32)) & ~m16
    q64 = (i // 32) != (j // 32)
    r = np.arange(4 * c)[:, None]
    bd = (r // c) == (cc // c)
    f = lambda a: jnp.asarray(a.astype(np.float32))
    return dict(tri=jnp.asarray(tri.astype(np.float32), BF16), trit=jnp.asarray(tri.T.astype(np.float32), BF16),
                incl=f(incl), strict=f(strict), eye=f(eye), m16=f(m16), q32=f(q32), q64=f(q64), bd=f(bd))


def _blockdiag(xcat, bd):
    return jnp.concatenate([xcat] * 4, axis=0) * bd


def _delta_kernel(q_ref, k_ref, v_ref, sm_ref, smt_ref, pr_ref, pc_ref, tri_ref, trit_ref, incl_ref, strict_ref,
                  eye_ref, m16_ref, q32_ref, q64_ref, bd_ref, o_ref, s_scr, g_scr, b_scr, gt_scr, *, reverse, dirn):
    @pl.when(pl.program_id(0) == 0)
    def _():
        s_scr[...] = jnp.zeros_like(s_scr)

    n_chunks = q_ref.shape[0] // CHUNK
    sm = sm_ref[:, 0:HEAD]
    g_scr[...] = -jnp.exp(pr_ref[0:1, :]) * _softplus(sm + pr_ref[1:2, :])
    b_scr[...] = _sigmoid(sm)
    gt = -jnp.exp(pc_ref[:, 0:1]) * _softplus(smt_ref[...] + pc_ref[:, 1:2])
    for c in range(n_chunks):
        gt_scr[c] = _dot_exact_rhs(gt[SM_A:SM_A + 2 * DELTA_HEADS, c * CHUNK:(c + 1) * CHUNK], trit_ref[...])
    last = 0 if reverse else CHUNK - 1
    incl = incl_ref[...] > 0.0
    strict = strict_ref[...] > 0.0
    bd = bd_ref[...]

    def chunk_body(ci, carry):
        cc = (n_chunks - 1 - ci) if reverse else ci
        r0 = pl.multiple_of(cc * CHUNK, CHUNK)
        rows = pl.ds(r0, CHUNK)
        gc = _dot_exact_lhs(tri_ref[...], g_scr[rows, :])
        beta = b_scr[rows, :]
        gct = gt_scr[cc]
        for grp in range(DELTA_HEADS // 4):
            a_list, att_list, rhs_list, misc = [], [], [], []
            for hh in range(4):
                h = grp * 4 + hh
                cols = slice(h * HEAD, (h + 1) * HEAD)
                la = SM_A + DELTA_HEADS * dirn + h
                lb = SM_B + DELTA_HEADS * dirn + h
                gcol = gc[:, la:la + 1]
                grow = gct[DELTA_HEADS * dirn + h:DELTA_HEADS * dirn + h + 1, :]
                bcol = beta[:, lb:lb + 1]
                q = q_ref[rows, cols]
                k = k_ref[rows, cols]
                v = v_ref[rows, cols]
                dec = jnp.exp(jnp.where(incl, gcol - grow, -jnp.inf))
                kb = k * bcol
                kbf = _bf(k)
                a_list.append(jnp.where(strict, _dot_nt(_bf(kb), kbf) * dec, 0.0))
                att_list.append(_dot_nt(_bf(q), kbf) * dec)
                eg = jnp.exp(gcol)
                rhs_list.append(jnp.concatenate([v * bcol, kb * eg], axis=1))
                g_last = gcol[last:last + 1, :]
                misc.append((cols, _bf(q * eg), _bf(k * jnp.exp(g_last - gcol)), jnp.exp(g_last)))
            acat = jnp.concatenate(a_list, axis=1)
            n1 = -(acat * m16_ref[...])
            p = eye_ref[...] + n1
            n2 = _dot3(n1, _blockdiag(n1, bd))
            p = p + _dot3(p, _blockdiag(n2, bd))
            n4 = _dot3(n2, _blockdiag(n2, bd))
            p = p + _dot3(p, _blockdiag(n4, bd))
            n8 = _dot3(n4, _blockdiag(n4, bd))
            p = p + _dot3(p, _blockdiag(n8, bd))
            x = _dot3(p, _blockdiag(acat * q32_ref[...], bd))
            p = p - _dot3(x, _blockdiag(p, bd))
            x = _dot3(p, _blockdiag(acat * q64_ref[...], bd))
            p = p - _dot3(x, _blockdiag(p, bd))
            sol = _dot3(_blockdiag(p, bd), jnp.concatenate(rhs_list, axis=0))
            for hh in range(4):
                h = grp * 4 + hh
                cols, qd, kd, egl = misc[hh]
                u = sol[hh * CHUNK:(hh + 1) * CHUNK, 0:HEAD]
                w = sol[hh * CHUNK:(hh + 1) * CHUNK, HEAD:2 * HEAD]
                s = s_scr[h]
                sb = _bf(s)
                v_new = u - _dot(_bf(w), sb)
                vnb = _bf(v_new)
                o_ref[rows, cols] = _dot(qd, sb) + _dot(_bf(att_list[hh]), vnb)
                s_scr[h] = s * egl + _dot_tn(kd, vnb)
        return carry

    lax.fori_loop(0, n_chunks, chunk_body, 0)


def _delta(qkv, z, zt, prow, pcol, reverse):
    ntok = z.shape[0]
    nblk = ntok // SCAN_ROWS
    cst = _delta_consts(reverse)
    names = ("tri", "trit", "incl", "strict", "eye", "m16", "q32", "q64", "bd")
    consts = [cst[n] for n in names]
    rb = functools.partial(_scan_block_index, nblk=nblk, reverse=reverse)
    full = lambda a: pl.BlockSpec(a.shape, lambda i: (0,) * a.ndim)
    return pl.pallas_call(
        functools.partial(_delta_kernel, reverse=reverse, dirn=1 if reverse else 0),
        out_shape=jax.ShapeDtypeStruct((ntok, DELTA_W), F32),
        grid=(nblk,),
        in_specs=[
            pl.BlockSpec((SCAN_ROWS, DELTA_W), lambda i: (rb(i), 0)),
            pl.BlockSpec((SCAN_ROWS, DELTA_W), lambda i: (rb(i), 1)),
            pl.BlockSpec((SCAN_ROWS, DELTA_W), lambda i: (rb(i), 2)),
            pl.BlockSpec((SCAN_ROWS, SM_W), lambda i: (rb(i), COL_SM // SM_W)),
            pl.BlockSpec((HEAD, SCAN_ROWS), lambda i: (0, rb(i))),
            full(prow), full(pcol)] + [full(a) for a in consts],
        out_specs=pl.BlockSpec((SCAN_ROWS, DELTA_W), lambda i: (rb(i), 0)),
        scratch_shapes=[pltpu.VMEM((DELTA_HEADS, HEAD, HEAD), F32),
                        pltpu.VMEM((SCAN_ROWS, HEAD), F32),
                        pltpu.VMEM((SCAN_ROWS, HEAD), F32),
                        pltpu.VMEM((SCAN_ROWS // CHUNK, 2 * DELTA_HEADS, CHUNK), F32)],
        compiler_params=_cparams(("arbitrary",)),
        name="delta_bwd" if reverse else "delta_fwd",
    )(qkv, qkv, qkv, z, zt, prow, pcol, *consts)


def _head_rms(x, w_ref, n_heads):
    parts = []
    for h in range(n_heads):
        xh = x[:, h * HEAD:(h + 1) * HEAD]
        parts.append(xh * lax.rsqrt(jnp.mean(xh * xh, axis=-1, keepdims=True) + RMS_EPS) * w_ref[...])
    return jnp.concatenate(parts, axis=1)


def _up_kernel(oaf_ref, oab_ref, od_ref, oef_ref, oeb_ref, gg_ref, eg_ref, ga_ref, gd_ref, ge_ref,
               nwa_ref, nwd_ref, nwe_ref, wa_ref, wd_ref, we_ref, y_ref, a_scr, d_scr, e_scr, *, lam_init):
    @pl.when(pl.program_id(1) == 0)
    def _():
        a_scr[...] = _bf(_head_rms(oaf_ref[...] + oab_ref[...], nwa_ref, GLA_HEADS) * _silu(gg_ref[...]))
        d_scr[...] = _bf(_head_rms(od_ref[...], nwd_ref, DIFF_HEADS) * (1.0 - lam_init))
        e_scr[...] = _bf(_head_rms(oef_ref[...] + oeb_ref[...], nwe_ref, DELTA_HEADS) * _silu(eg_ref[...]))

    y = _sigmoid(ga_ref[...]) * _dot(a_scr[...], wa_ref[...])
    y = y + _sigmoid(gd_ref[...]) * _dot(d_scr[...], wd_ref[...])
    y = y + _sigmoid(ge_ref[...]) * _dot(e_scr[...], we_ref[...])
    y_ref[...] = _bf(y)


def _up(oaf, oab, od, oef, oeb, z, nwa, nwd, nwe, wa, wd, we, lam_init, tm):
    ntok = z.shape[0]
    tn = UP_TN
    rows = lambda w: pl.BlockSpec((tm, w), lambda i, j: (i, 0))
    zcol = lambda col, w: pl.BlockSpec((tm, w), lambda i, j: (i, col // w))
    gate = lambda b: pl.BlockSpec((tm, tn), lambda i, j: (i, (COL_MG + b * D_MODEL) // tn + j))
    vec = pl.BlockSpec((1, HEAD), lambda i, j: (0, 0))
    wspec = lambda k: pl.BlockSpec((k, tn), lambda i, j: (0, j))
    return pl.pallas_call(
        functools.partial(_up_kernel, lam_init=lam_init),
        out_shape=jax.ShapeDtypeStruct((ntok, D_MODEL), BF16),
        grid=(ntok // tm, D_MODEL // tn),
        in_specs=[rows(GLA_W), rows(GLA_W), rows(DIFF_W), rows(DELTA_W), rows(DELTA_W),
                  zcol(COL_GG, GLA_W), zcol(COL_EG, DELTA_W), gate(0), gate(1), gate(2),
                  vec, vec, vec, wspec(GLA_W), wspec(DIFF_W), wspec(DELTA_W)],
        out_specs=pl.BlockSpec((tm, tn), lambda i, j: (i, j)),
        scratch_shapes=[pltpu.VMEM((tm, GLA_W), BF16), pltpu.VMEM((tm, DIFF_W), BF16), pltpu.VMEM((tm, DELTA_W), BF16)],
        compiler_params=_cparams(("arbitrary", "arbitrary")),
        name="up_merge",
    )(oaf, oab, od, oef, oeb, z, z, z, z, z, nwa.reshape(1, HEAD), nwd.reshape(1, HEAD), nwe.reshape(1, HEAD),
      wa, wd, we)


def _oproj_kernel(y_ref, x_ref, wo_ref, nw_ref, gate_ref, o_ref):
    tm = x_ref.shape[0]
    is_ctx = _row_is_ctx(pl.program_id(0) * tm, tm)
    t = _dot(y_ref[...], wo_ref[...])
    o_ref[...] = x_ref[...] + _pick_mod(is_ctx, gate_ref) * _rms(t, nw_ref[...])


def _oproj(y, xc, wo, norm_w, mods, tm):
    ntok, d = xc.shape
    return pl.pallas_call(
        _oproj_kernel,
        out_shape=jax.ShapeDtypeStruct((ntok, d), F32),
        grid=(ntok // tm,),
        in_specs=[
            pl.BlockSpec((tm, d), lambda i: (i, 0)),
            pl.BlockSpec((tm, d), lambda i: (i, 0)),
            pl.BlockSpec((d, d), lambda i: (0, 0)),
            pl.BlockSpec((1, d), lambda i: (0, 0)),
            pl.BlockSpec((8, d), lambda i: (0, 2)),
        ],
        out_specs=pl.BlockSpec((tm, d), lambda i: (i, 0)),
        compiler_params=_cparams(("arbitrary",)),
        name="out_proj",
    )(y, xc, wo, norm_w.reshape(1, d), mods)


def _ffn_kernel(x_ref, nw_ref, sh_ref, sc_ref, gate_ref, pw_ref, w1_ref, w3_ref, w2_ref, o_ref, h_scr, acc_scr):
    j = pl.program_id(1)
    tm = x_ref.shape[0]
    is_ctx = _row_is_ctx(pl.program_id(0) * tm, tm)

    @pl.when(j == 0)
    def _():
        h = _rms(x_ref[...], nw_ref[...])
        h = h * (1.0 + _pick_mod(is_ctx, sc_ref)) + _pick_mod(is_ctx, sh_ref)
        h_scr[...] = _bf(h)
        acc_scr[...] = jnp.zeros_like(acc_scr)

    h = h_scr[...]
    u = _silu(_dot(h, w1_ref[...])) * _dot(h, w3_ref[...])
    acc_scr[...] += _dot(_bf(u), w2_ref[...])

    @pl.when(j == pl.num_programs(1) - 1)
    def _():
        o_ref[...] = x_ref[...] + _pick_mod(is_ctx, gate_ref) * _rms(acc_scr[...], pw_ref[...])


def _ffn(xc, pre_w, post_w, mods, w1, w3, w2, tm):
    ntok, d = xc.shape
    dff = w1.shape[1]
    return pl.pallas_call(
        _ffn_kernel,
        out_shape=jax.ShapeDtypeStruct((ntok, d), F32),
        grid=(ntok // tm, dff // FF_TN),
        in_specs=[
            pl.BlockSpec((tm, d), lambda i, j: (i, 0), pipeline_mode=pl.Buffered(1)),
            pl.BlockSpec((1, d), lambda i, j: (0, 0)),
            pl.BlockSpec((8, d), lambda i, j: (0, 3)),
            pl.BlockSpec((8, d), lambda i, j: (0, 4)),
            pl.BlockSpec((8, d), lambda i, j: (0, 5)),
            pl.BlockSpec((1, d), lambda i, j: (0, 0)),
            pl.BlockSpec((d, FF_TN), lambda i, j: (0, j)),
            pl.BlockSpec((d, FF_TN), lambda i, j: (0, j)),
            pl.BlockSpec((FF_TN, d), lambda i, j: (j, 0)),
        ],
        out_specs=pl.BlockSpec((tm, d), lambda i, j: (i, 0), pipeline_mode=pl.Buffered(1)),
        scratch_shapes=[pltpu.VMEM((tm, d), BF16), pltpu.VMEM((tm, d), F32)],
        compiler_params=_cparams(("arbitrary", "arbitrary")),
        name="ffn",
    )(xc, pre_w.reshape(1, d), mods, mods, mods, post_w.reshape(1, d), w1, w3, w2)


def _permute_w_in(w):
    d = w.shape[0]
    pieces = [w[:, 0:1536], w[:, 1568:2080], w[:, 6720:7744], w[:, 3616:6688], w[:, 2080:3616], w[:, 7744:13888],
              w[:, 1536:1568], w[:, 6688:6720], jnp.zeros((d, SM_W - 64), w.dtype)]
    return _bf(jnp.concatenate(pieces, axis=1))


def _rope_tables(seq):
    rows = seq // GRID_W
    row_ids = jnp.repeat(jnp.arange(rows, dtype=F32), GRID_W)
    col_ids = jnp.tile(jnp.arange(GRID_W, dtype=F32), rows)
    half = DIFF_DQK // 2
    inv = 1.0 / (ROPE_BASE ** (jnp.arange(0, half, 2, dtype=F32) / half))
    ang_r = row_ids[:, None] * inv
    ang_c = col_ids[:, None] * inv
    cr, sr, cc, sc = jnp.cos(ang_r), jnp.sin(ang_r), jnp.cos(ang_c), jnp.sin(ang_c)
    cos_lat = jnp.tile(jnp.concatenate([cr, cr, cc, cc], axis=1), (1, 2))
    sin_lat = jnp.tile(jnp.concatenate([-sr, sr, -sc, sc], axis=1), (1, 2))
    cos_t = jnp.concatenate([jnp.ones((CTX_LEN, HEAD), F32), cos_lat], axis=0)
    sin_t = jnp.concatenate([jnp.zeros((CTX_LEN, HEAD), F32), sin_lat], axis=0)
    return cos_t, sin_t


def _lane_params(a_log, dt_bias):
    flat_a = a_log.reshape(-1)
    flat_b = dt_bias.reshape(-1)
    n = flat_a.shape[0]
    pa = jnp.zeros((HEAD,), F32).at[SM_A:SM_A + n].set(flat_a)
    pb = jnp.zeros((HEAD,), F32).at[SM_A:SM_A + n].set(flat_b)
    prow = jnp.zeros((8, HEAD), F32).at[0].set(pa).at[1].set(pb)
    pcol = jnp.zeros((HEAD, HEAD), F32).at[:, 0].set(pa).at[:, 1].set(pb)
    return prow, pcol


def _gla_gate_weights(w2, bias, dirn):
    wpad = jnp.zeros((HEAD, GLA_W), F32).at[GLA_GATE_RANK * dirn:GLA_GATE_RANK * (dirn + 1)].set(w2[dirn])
    return _bf(wpad), bias[dirn].reshape(1, GLA_W)


def kernel(x, c, ctx, c_ctx, ada_w, ada_b, mix_pre_w, mix_post_w, ffn_pre_w, ffn_post_w, w_in, gla_gate_w2, gla_gate_b, gla_norm_w, diff_lambda, diff_norm_w, delta_conv_w, delta_a_log, delta_dt_bias, delta_norm_w, w_up_gla, w_up_diff, w_up_delta, w_o, ffn_w1, ffn_w3, ffn_w2):
    seq = x.shape[1]
    ntok = CTX_LEN + seq
    tm = ROW_TILE if ntok % ROW_TILE == 0 else SCAN_ROWS
    xc = jnp.concatenate([ctx[0], x[0]], axis=0)
    cond = jnp.concatenate([c, c_ctx[None, :], jnp.zeros((COND_ROWS - 2, D_MODEL), F32)], axis=0)
    mods_all = _ada(_bf(jax.nn.silu(cond)), ada_w, ada_b)
    cos_t, sin_t = _rope_tables(seq)
    kv_tile = ROW_TILE if ntok % ROW_TILE == 0 else SCAN_ROWS

    for layer in range(DEPTH):
        lam_init = 0.8 - 0.6 * math.exp(-0.3 * layer)
        mods = mods_all[layer]
        z = _inproj(xc, mix_pre_w[layer], mods, _permute_w_in(w_in[layer]), tm)

        oa = []
        for dirn in range(2):
            w2pad, bias = _gla_gate_weights(gla_gate_w2[layer], gla_gate_b[layer], dirn)
            oa.append(_gla(z, w2pad, bias, reverse=bool(dirn)))

        qr, kr, vr = _rope(z, cos_t, sin_t)
        od_ctx = _attn(diff_lambda[layer], qr, kr, vr, 0, CTX_LEN, CTX_LEN, CTX_LEN, lam_init)
        od_lat = _attn(diff_lambda[layer], qr, kr, vr, CTX_LEN, seq, ntok, kv_tile, lam_init)
        od = jnp.concatenate([od_ctx, od_lat], axis=0)

        qkv = _conv(z, delta_conv_w[layer])
        zt = z[:, COL_SM:COL_SM + HEAD].T
        prow, pcol = _lane_params(delta_a_log[layer], delta_dt_bias[layer])
        oe = [_delta(qkv, z, zt, prow, pcol, reverse=bool(dirn)) for dirn in range(2)]

        y = _up(oa[0], oa[1], od, oe[0], oe[1], z, gla_norm_w[layer], diff_norm_w[layer], delta_norm_w[layer],
                _bf(w_up_gla[layer]), _bf(w_up_diff[layer]), _bf(w_up_delta[layer]), lam_init, tm)
        xc = _oproj(y, xc, _bf(w_o[layer]), mix_post_w[layer], mods, tm)
        xc = _ffn(xc, ffn_pre_w[layer], ffn_post_w[layer], mods, _bf(ffn_w1[layer]), _bf(ffn_w3[layer]),
                  _bf(ffn_w2[layer]), tm)
    return xc[CTX_LEN:][None]
```

```python
import functools
import math

import numpy as np
import jax
import jax.numpy as jnp
from jax import lax
from jax.experimental import pallas as pl
from jax.experimental.pallas import tpu as pltpu

F32 = jnp.float32
BF16 = jnp.bfloat16

D_MODEL = 2048
DEPTH = 2
GRID_W = 64
CTX_LEN = 256
HEAD = 128
GLA_HEADS = 4
GLA_GATE_RANK = 16
GLA_GATE_NORM = 16.0
DIFF_HEADS = 4
DIFF_DQK = 64
DELTA_HEADS = 8
DELTA_CONV = 5
CHUNK = 64
ROPE_BASE = 10000.0
RMS_EPS = 1e-6
L2_EPS = 1e-6
D_FF = 5632
GLA_W = GLA_HEADS * HEAD
DIFF_W = DIFF_HEADS * HEAD
DELTA_W = DELTA_HEADS * HEAD

COL_GQ, COL_GK, COL_GV, COL_GG = 0, 512, 1024, 1536
COL_EG = 2048
COL_EQ, COL_EK, COL_EV = 3072, 4096, 5120
COL_DQ, COL_DK, COL_DV = 6144, 6656, 7168
COL_MG = 7680
COL_SM = 13824
SM_W = 256
NZ = COL_SM + SM_W
SM_A = 32
SM_B = 48

ROW_TILE = 768
SCAN_ROWS = 256
IN_TN = 1280
FF_TN = 512
UP_TN = 512
ATT_TQ = 256
COND_ROWS = 16
VMEM_LIMIT = 56 * 1024 * 1024


def _cparams(sem, vmem=VMEM_LIMIT):
    return pltpu.CompilerParams(dimension_semantics=sem, vmem_limit_bytes=vmem)


def _bf(x):
    return x.astype(BF16)


def _dot(a, b):
    return jnp.dot(a, b, preferred_element_type=F32)


def _dot_nt(a, b):
    return lax.dot_general(a, b, (((1,), (1,)), ((), ())), preferred_element_type=F32)


def _dot_tn(a, b):
    return lax.dot_general(a, b, (((0,), (0,)), ((), ())), preferred_element_type=F32)


def _split3(x):
    hi = _bf(x)
    r = x - hi.astype(F32)
    mid = _bf(r)
    lo = _bf(r - mid.astype(F32))
    return hi, mid, lo


def _dot_exact_lhs(m_bf, x):
    hi, mid, lo = _split3(x)
    return _dot(m_bf, hi) + _dot(m_bf, mid) + _dot(m_bf, lo)


def _dot_exact_rhs(x, m_bf):
    hi, mid, lo = _split3(x)
    return _dot(hi, m_bf) + _dot(mid, m_bf) + _dot(lo, m_bf)


def _dot3(a, b):
    ah = _bf(a)
    al = _bf(a - ah.astype(F32))
    bh = _bf(b)
    bl = _bf(b - bh.astype(F32))
    return _dot(ah, bh) + _dot(ah, bl) + _dot(al, bh)


def _sigmoid(x):
    return 1.0 / (1.0 + jnp.exp(-x))


def _silu(x):
    return x * _sigmoid(x)


def _softplus(x):
    return jnp.maximum(x, 0.0) + jnp.log1p(jnp.exp(-jnp.abs(x)))


def _log_sigmoid(x):
    return jnp.minimum(x, 0.0) - jnp.log1p(jnp.exp(-jnp.abs(x)))


def _rms(x, w):
    return x * lax.rsqrt(jnp.mean(x * x, axis=-1, keepdims=True) + RMS_EPS) * w


def _row_is_ctx(row0, n):
    rows = row0 + lax.broadcasted_iota(jnp.int32, (n, 1), 0)
    return rows < CTX_LEN


def _pick_mod(is_ctx, mod_ref):
    return jnp.where(is_ctx, mod_ref[1:2, :], mod_ref[0:1, :])


def _ada_kernel(s_ref, w_ref, b_ref, o_ref):
    o_ref[0] = _dot(s_ref[...], _bf(w_ref[0])) + b_ref[0]


def _ada(s_bf, ada_w, ada_b):
    depth, d, n6 = ada_w.shape
    tn = 1024
    return pl.pallas_call(
        _ada_kernel,
        out_shape=jax.ShapeDtypeStruct((depth, COND_ROWS, n6), F32),
        grid=(depth, n6 // tn),
        in_specs=[
            pl.BlockSpec((COND_ROWS, d), lambda l, j: (0, 0)),
            pl.BlockSpec((1, d, tn), lambda l, j: (l, 0, j)),
            pl.BlockSpec((1, 1, tn), lambda l, j: (l, 0, j)),
        ],
        out_specs=pl.BlockSpec((1, COND_ROWS, tn), lambda l, j: (l, 0, j)),
        compiler_params=_cparams(("arbitrary", "arbitrary")),
        name="ada",
    )(s_bf, ada_w, ada_b.reshape(depth, 1, n6))


def _inproj_kernel(x_ref, nw_ref, sh_ref, sc_ref, w_ref, o_ref, h_scr):
    @pl.when(pl.program_id(1) == 0)
    def _():
        tm = x_ref.shape[0]
        is_ctx = _row_is_ctx(pl.program_id(0) * tm, tm)
        h = _rms(x_ref[...], nw_ref[...])
        h = h * (1.0 + _pick_mod(is_ctx, sc_ref)) + _pick_mod(is_ctx, sh_ref)
        h_scr[...] = _bf(h)

    o_ref[...] = _dot(h_scr[...], w_ref[...])


def _inproj(xc, norm_w, mods, w_bf, tm):
    ntok, d = xc.shape
    nz = w_bf.shape[1]
    return pl.pallas_call(
        _inproj_kernel,
        out_shape=jax.ShapeDtypeStruct((ntok, nz), F32),
        grid=(ntok // tm, nz // IN_TN),
        in_specs=[
            pl.BlockSpec((tm, d), lambda i, j: (i, 0)),
            pl.BlockSpec((1, d), lambda i, j: (0, 0)),
            pl.BlockSpec((8, d), lambda i, j: (0, 0)),
            pl.BlockSpec((8, d), lambda i, j: (0, 1)),
            pl.BlockSpec((d, IN_TN), lambda i, j: (0, j)),
        ],
        out_specs=pl.BlockSpec((tm, IN_TN), lambda i, j: (i, j)),
        scratch_shapes=[pltpu.VMEM((tm, d), BF16)],
        compiler_params=_cparams(("arbitrary", "arbitrary")),
        name="inproj",
    )(xc, norm_w.reshape(1, d), mods, mods, w_bf)


def _gla_consts(reverse):
    c = CHUNK
    i = np.arange(c)[:, None]
    t = np.arange(c)[None, :]
    if not reverse:
        mats = [t <= i, t > i]
    else:
        mats = [t >= i, t < i]
    masks = [i == t]
    for m in (32, 16, 8, 4, 2, 1):
        p = (i // (2 * m)) * (2 * m) + m - 1
        second = ((i // m) % 2) == 1
        same = (i // (2 * m)) == (t // (2 * m))
        t_second = ((t // m) % 2) == 1
        if not reverse:
            w = np.where(second, (t > p) & (t <= i), (t > i) & (t <= p))
            pm = same & second & ~t_second
        else:
            w = np.where(second, (t >= p + 1) & (t <= i - 1), (t >= i) & (t <= p))
            pm = same & ~second & t_second
        mats.append(w)
        masks.append(pm)
    wall = np.concatenate([m_.astype(np.float32) for m_ in mats], axis=0)
    pmask = np.stack([m_.astype(np.float32) for m_ in masks], axis=0)
    return jnp.asarray(wall, BF16), jnp.asarray(pmask, F32)


def _gla_kernel(q_ref, k_ref, v_ref, sm_ref, w2_ref, b_ref, wall_ref, pm_ref, o_ref, st_scr, g_scr,
                *, reverse):
    @pl.when(pl.program_id(0) == 0)
    def _():
        st_scr[...] = jnp.zeros_like(st_scr)

    x = _dot(_bf(sm_ref[:, 0:HEAD]), w2_ref[...]) + b_ref[...]
    g_scr[...] = _log_sigmoid(x) * (1.0 / GLA_GATE_NORM)
    n_chunks = q_ref.shape[0] // CHUNK
    last = 0 if reverse else CHUNK - 1
    scale = HEAD ** -0.5

    units = []
    for cc in range(n_chunks):
        rows = slice(cc * CHUNK, (cc + 1) * CHUNK)
        e_all = jnp.exp(_dot_exact_lhs(wall_ref[...], g_scr[rows, :]))
        for h in range(GLA_HEADS):
            cols = slice(h * HEAD, (h + 1) * HEAD)
            units.append((cc, h, q_ref[rows, cols] * scale, k_ref[rows, cols], _bf(v_ref[rows, cols]), e_all[:, cols]))
    att = [jnp.where(pm_ref[0] > 0.0, _dot_nt(_bf(q), _bf(k)), 0.0) for (_, _, q, k, _, _) in units]
    for lv in range(6):
        lvl = slice((2 + lv) * CHUNK, (3 + lv) * CHUNK)
        pm = pm_ref[lv + 1] > 0.0
        att = [a + jnp.where(pm, _dot_nt(_bf(q * e[lvl]), _bf(k * e[lvl])), 0.0)
               for a, (_, _, q, k, _, e) in zip(att, units)]
    o_intra = {(cc, h): _dot(_bf(a), vb) for a, (cc, h, _, _, vb, _) in zip(att, units)}
    q_in = {(cc, h): _bf(q * e[0:CHUNK]) for (cc, h, q, _, _, e) in units}
    k_st = {(cc, h): _bf(k * e[CHUNK:2 * CHUNK]) for (cc, h, _, k, _, e) in units}
    v_bf = {(cc, h): vb for (cc, h, _, _, vb, _) in units}
    e_last = {(cc, h): e[last:last + 1, :] for (cc, h, _, _, _, e) in units}

    st = [st_scr[h] for h in range(GLA_HEADS)]
    for ci in range(n_chunks):
        cc = (n_chunks - 1 - ci) if reverse else ci
        for h in range(GLA_HEADS):
            o_ref[cc * CHUNK:(cc + 1) * CHUNK, h * HEAD:(h + 1) * HEAD] = (
                _dot_nt(q_in[(cc, h)], _bf(st[h])) + o_intra[(cc, h)])
            st[h] = st[h] * e_last[(cc, h)] + _dot_tn(v_bf[(cc, h)], k_st[(cc, h)])
    for h in range(GLA_HEADS):
        st_scr[h] = st[h]


def _scan_block_index(i, nblk, reverse):
    if not reverse:
        return i
    nctx = CTX_LEN // SCAN_ROWS
    return jnp.where(i < nctx, nctx - 1 - i, nblk - 1 + nctx - i)


def _gla(z, w2pad, bias, reverse):
    ntok = z.shape[0]
    nblk = ntok // SCAN_ROWS
    wall, pmask = _gla_consts(reverse)
    rb = functools.partial(_scan_block_index, nblk=nblk, reverse=reverse)
    return pl.pallas_call(
        functools.partial(_gla_kernel, reverse=reverse),
        out_shape=jax.ShapeDtypeStruct((ntok, GLA_W), F32),
        grid=(nblk,),
        in_specs=[
            pl.BlockSpec((SCAN_ROWS, GLA_W), lambda i: (rb(i), COL_GQ // GLA_W)),
            pl.BlockSpec((SCAN_ROWS, GLA_W), lambda i: (rb(i), COL_GK // GLA_W)),
            pl.BlockSpec((SCAN_ROWS, GLA_W), lambda i: (rb(i), COL_GV // GLA_W)),
            pl.BlockSpec((SCAN_ROWS, SM_W), lambda i: (rb(i), COL_SM // SM_W)),
            pl.BlockSpec((HEAD, GLA_W), lambda i: (0, 0)),
            pl.BlockSpec((1, GLA_W), lambda i: (0, 0)),
            pl.BlockSpec(wall.shape, lambda i: (0, 0)),
            pl.BlockSpec(pmask.shape, lambda i: (0, 0, 0)),
        ],
        out_specs=pl.BlockSpec((SCAN_ROWS, GLA_W), lambda i: (rb(i), 0)),
        scratch_shapes=[pltpu.VMEM((GLA_HEADS, HEAD, HEAD), F32), pltpu.VMEM((SCAN_ROWS, GLA_W), F32)],
        compiler_params=_cparams(("arbitrary",)),
        name="gla_bwd" if reverse else "gla_fwd",
    )(z, z, z, z, w2pad, bias, wall, pmask)


def _rope_kernel(q_ref, k_ref, v_ref, cos_ref, sin_ref, qo_ref, ko_ref, vo_ref):
    w = q_ref.shape[1]
    cos = jnp.concatenate([cos_ref[...]] * DIFF_HEADS, axis=1)
    sin = jnp.concatenate([sin_ref[...]] * DIFF_HEADS, axis=1)
    lane = lax.broadcasted_iota(jnp.int32, q_ref.shape, 1)
    first = (lane % 32) < 16

    def rot(x):
        swapped = jnp.where(first, pltpu.roll(x, w - 16, axis=1), pltpu.roll(x, 16, axis=1))
        return x * cos + swapped * sin

    qo_ref[...] = _bf(rot(q_ref[...]) * (DIFF_DQK ** -0.5 * math.log2(math.e)))
    ko_ref[...] = _bf(rot(k_ref[...]))
    lane_h = lax.broadcasted_iota(jnp.int32, (q_ref.shape[0], HEAD), 1)
    ones_col = jnp.where(lane_h == 0, 1.0, 0.0).astype(BF16)
    for h in range(DIFF_HEADS):
        vo_ref[:, 2 * h * HEAD:(2 * h + 1) * HEAD] = _bf(v_ref[:, h * HEAD:(h + 1) * HEAD])
        vo_ref[:, (2 * h + 1) * HEAD:(2 * h + 2) * HEAD] = ones_col


def _rope(z, cos_t, sin_t):
    ntok = z.shape[0]
    tm = SCAN_ROWS
    spec = lambda col: pl.BlockSpec((tm, DIFF_W), lambda i: (i, col // DIFF_W))
    out = jax.ShapeDtypeStruct((ntok, DIFF_W), BF16)
    return pl.pallas_call(
        _rope_kernel,
        out_shape=(out, out, jax.ShapeDtypeStruct((ntok, 2 * DIFF_W), BF16)),
        grid=(ntok // tm,),
        in_specs=[spec(COL_DQ), spec(COL_DK), spec(COL_DV),
                  pl.BlockSpec((tm, HEAD), lambda i: (i, 0)),
                  pl.BlockSpec((tm, HEAD), lambda i: (i, 0))],
        out_specs=(pl.BlockSpec((tm, DIFF_W), lambda i: (i, 0)), pl.BlockSpec((tm, DIFF_W), lambda i: (i, 0)),
                   pl.BlockSpec((tm, 2 * DIFF_W), lambda i: (i, 0))),
        compiler_params=_cparams(("arbitrary",)),
        name="rope",
    )(z, z, z, cos_t, sin_t)


def _attn_kernel(lam_ref, q_ref, k_ref, v_ref, o_ref, *, n_kv, tk, lam_init):
    tq = q_ref.shape[0]
    q = q_ref[...]
    lane = lax.broadcasted_iota(jnp.int32, q.shape, 1)
    zero = jnp.zeros_like(q)
    q2 = jnp.concatenate([jnp.where(lane < DIFF_DQK, q, zero), jnp.where(lane >= DIFF_DQK, q, zero)], axis=0)

    m = jnp.full((2 * tq, 1), -jnp.inf, F32)
    acc = jnp.zeros((2 * tq, 2 * HEAD), F32)
    for c in range(n_kv):
        rows = slice(c * tk, (c + 1) * tk)
        s = _dot_nt(q2, k_ref[rows, :])
        m_new = jnp.maximum(m, jnp.max(s, axis=-1, keepdims=True))
        acc = jnp.exp2(m - m_new) * acc + _dot(_bf(jnp.exp2(s - m_new)), v_ref[rows, :])
        m = m_new
    o = acc[:, 0:HEAD] / acc[:, HEAD:HEAD + 1]
    lp = lam_ref[...]
    lam = (jnp.exp(jnp.sum(lp[0:1] * lp[1:2], axis=-1, keepdims=True))
           - jnp.exp(jnp.sum(lp[2:3] * lp[3:4], axis=-1, keepdims=True)) + lam_init)
    o_ref[...] = o[0:tq] - lam * o[tq:2 * tq]


def _attn(lam_p, q, k, v, q_row0, n_q, kv_rows, tk, lam_init):
    tq = ATT_TQ
    qb0 = q_row0 // tq
    return pl.pallas_call(
        functools.partial(_attn_kernel, n_kv=kv_rows // tk, tk=tk, lam_init=lam_init),
        out_shape=jax.ShapeDtypeStruct((n_q, DIFF_W), F32),
        grid=(DIFF_HEADS, n_q // tq),
        in_specs=[
            pl.BlockSpec(lam_p.shape, lambda h, i: (0, 0)),
            pl.BlockSpec((tq, HEAD), lambda h, i: (i + qb0, h)),
            pl.BlockSpec((kv_rows, HEAD), lambda h, i: (0, h)),
            pl.BlockSpec((kv_rows, 2 * HEAD), lambda h, i: (0, h)),
        ],
        out_specs=pl.BlockSpec((tq, HEAD), lambda h, i: (i, h)),
        compiler_params=_cparams(("arbitrary", "arbitrary")),
        name="diff_attn",
    )(lam_p, q, k, v)


def _conv_kernel(xp_ref, x_ref, xn_ref, w_ref, o_ref):
    i = pl.program_id(0)
    nblk = pl.num_programs(0)
    nctx = CTX_LEN // SCAN_ROWS
    tm = x_ref.shape[0]
    pad = DELTA_CONV // 2
    has_prev = jnp.logical_and(i != 0, i != nctx)
    has_next = jnp.logical_and(i != nctx - 1, i != nblk - 1)
    prev = jnp.where(has_prev, xp_ref[...], 0.0)
    nxt = jnp.where(has_next, xn_ref[...], 0.0)
    xe = jnp.concatenate([prev, x_ref[...], nxt], axis=0)
    acc = None
    for j in range(DELTA_CONV):
        off = 8 + j - pad
        term = xe[off:off + tm, :] * w_ref[j:j + 1, :]
        acc = term if acc is None else acc + term
    y = _silu(acc)
    jc = pl.program_id(1)
    q_blocks = DELTA_W // x_ref.shape[1]
    is_qk = jc < 2 * q_blocks
    post = jnp.where(jc < q_blocks, HEAD ** -0.5, 1.0)
    for hh in range(x_ref.shape[1] // HEAD):
        cols = slice(hh * HEAD, (hh + 1) * HEAD)
        yh = y[:, cols]
        yn = yh * lax.rsqrt(jnp.sum(yh * yh, axis=-1, keepdims=True) + L2_EPS) * post
        o_ref[:, cols] = jnp.where(is_qk, yn, yh)


def _conv(z, conv_w):
    ntok = z.shape[0]
    tm = SCAN_ROWS
    width = 3 * DELTA_W
    nb8 = ntok // 8
    r8 = tm // 8
    cw = 512
    ncb = width // cw
    c0 = COL_EQ // cw
    return pl.pallas_call(
        _conv_kernel,
        out_shape=jax.ShapeDtypeStruct((ntok, width), F32),
        grid=(ntok // tm, ncb),
        in_specs=[
            pl.BlockSpec((8, cw), lambda i, j: (jnp.maximum(i * r8 - 1, 0), c0 + j)),
            pl.BlockSpec((tm, cw), lambda i, j: (i, c0 + j)),
            pl.BlockSpec((8, cw), lambda i, j: (jnp.minimum((i + 1) * r8, nb8 - 1), c0 + j)),
            pl.BlockSpec((DELTA_CONV, cw), lambda i, j: (0, j)),
        ],
        out_specs=pl.BlockSpec((tm, cw), lambda i, j: (i, j)),
        compiler_params=_cparams(("arbitrary", "arbitrary")),
        name="delta_conv",
    )(z, z, z, conv_w)


def _delta_consts(reverse):
    c = CHUNK
    i = np.arange(c)[:, None]
    t = np.arange(c)[None, :]
    tri = (t >= i) if reverse else (t <= i)
    incl = tri
    strict = (t > i) if reverse else (t < i)
    cc = np.arange(4 * c)[None, :]
    j = cc % c
    eye = (i == j)
    m16 = (i // 16) == (j // 16)
    q32 = ((i // 32) == (j // 32)) & ~m16
    q64 = (i // 32) != (j // 32)
    r = np.arange(4 * c)[:, None]
    bd = (r // c) == (cc // c)
    f = lambda a: jnp.asarray(a.astype(np.float32))
    return dict(tri=jnp.asarray(tri.astype(np.float32), BF16), trit=jnp.asarray(tri.T.astype(np.float32), BF16),
                incl=f(incl), strict=f(strict), eye=f(eye), m16=f(m16), q32=f(q32), q64=f(q64),
                bd=jnp.asarray(bd.astype(np.float32), BF16))


def _blockdiag(xcat, bd):
    return jnp.concatenate([xcat] * 4, axis=0) * bd


def _delta_kernel(q_ref, k_ref, v_ref, sm_ref, smt_ref, pr_ref, pc_ref, tri_ref, trit_ref, incl_ref, strict_ref,
                  eye_ref, m16_ref, q32_ref, q64_ref, bd_ref, o_ref, s_scr, g_scr, b_scr, gt_scr, *, reverse, dirn):
    @pl.when(pl.program_id(0) == 0)
    def _():
        s_scr[...] = jnp.zeros_like(s_scr)

    n_chunks = q_ref.shape[0] // CHUNK
    sm = sm_ref[:, 0:HEAD]
    g_scr[...] = -jnp.exp(pr_ref[0:1, :]) * _softplus(sm + pr_ref[1:2, :])
    b_scr[...] = _sigmoid(sm)
    gt = -jnp.exp(pc_ref[:, 0:1]) * _softplus(smt_ref[...] + pc_ref[:, 1:2])
    for c in range(n_chunks):
        gt_scr[c] = _dot_exact_rhs(gt[SM_A:SM_A + 2 * DELTA_HEADS, c * CHUNK:(c + 1) * CHUNK], trit_ref[...])
    last = 0 if reverse else CHUNK - 1
    incl = incl_ref[...] > 0.0
    strict = strict_ref[...] > 0.0
    bd = bd_ref[...]

    def mm(a, b):
        return _dot(_bf(a), _blockdiag(_bf(b), bd))

    local = {}
    units = []
    for cc in range(n_chunks):
        rows = slice(cc * CHUNK, (cc + 1) * CHUNK)
        gc = _dot_exact_lhs(tri_ref[...], g_scr[rows, :])
        beta = b_scr[rows, :]
        gct = gt_scr[cc]
        for grp in range(DELTA_HEADS // 4):
            a_list, att_list, rhs_list, misc = [], [], [], []
            for hh in range(4):
                h = grp * 4 + hh
                cols = slice(h * HEAD, (h + 1) * HEAD)
                la = SM_A + DELTA_HEADS * dirn + h
                lb = SM_B + DELTA_HEADS * dirn + h
                gcol = gc[:, la:la + 1]
                grow = gct[DELTA_HEADS * dirn + h:DELTA_HEADS * dirn + h + 1, :]
                bcol = beta[:, lb:lb + 1]
                q = q_ref[rows, cols]
                k = k_ref[rows, cols]
                v = v_ref[rows, cols]
                dec = jnp.exp(jnp.where(incl, gcol - grow, -jnp.inf))
                kb = k * bcol
                kbf = _bf(k)
                a_list.append(jnp.where(strict, _dot_nt(_bf(kb), kbf) * dec, 0.0))
                att_list.append(_bf(_dot_nt(_bf(q), kbf) * dec))
                eg = jnp.exp(gcol)
                rhs_list.append(_bf(jnp.concatenate([v * bcol, kb * eg], axis=1)))
                g_last = gcol[last:last + 1, :]
                misc.append((_bf(q * eg), _bf(k * jnp.exp(g_last - gcol)), jnp.exp(g_last)))
            units.append((cc, grp, jnp.concatenate(a_list, axis=1), att_list, jnp.concatenate(rhs_list, axis=0), misc))

    acat = [u_[2] for u_ in units]
    n = [-(a * m16_ref[...]) for a in acat]
    p = [eye_ref[...] + n_ for n_ in n]
    for _ in range(3):
        n = [mm(n_, n_) for n_ in n]
        p = [p_ + mm(p_, n_) for p_, n_ in zip(p, n)]
    for q_ref_ in (q32_ref, q64_ref):
        x = [mm(p_, a * q_ref_[...]) for p_, a in zip(p, acat)]
        p = [p_ - mm(x_, p_) for p_, x_ in zip(p, x)]
    sol = [_dot(_blockdiag(_bf(p_), bd), u_[4]) for p_, u_ in zip(p, units)]
    for (cc, grp, _, att_list, _, misc), sol_ in zip(units, sol):
        for hh in range(4):
            u = sol_[hh * CHUNK:(hh + 1) * CHUNK, 0:HEAD]
            w = _bf(sol_[hh * CHUNK:(hh + 1) * CHUNK, HEAD:2 * HEAD])
            local[(cc, grp * 4 + hh)] = (u, w, att_list[hh]) + misc[hh]

    s = [s_scr[h] for h in range(DELTA_HEADS)]
    for ci in range(n_chunks):
        cc = (n_chunks - 1 - ci) if reverse else ci
        sb = [_bf(s_) for s_ in s]
        vnb = [_bf(local[(cc, h)][0] - _dot(local[(cc, h)][1], sb[h])) for h in range(DELTA_HEADS)]
        for h in range(DELTA_HEADS):
            _, _, att, qd, kd, egl = local[(cc, h)]
            o_ref[cc * CHUNK:(cc + 1) * CHUNK, h * HEAD:(h + 1) * HEAD] = _dot(qd, sb[h]) + _dot(att, vnb[h])
            s[h] = s[h] * egl + _dot_tn(kd, vnb[h])
    for h in range(DELTA_HEADS):
        s_scr[h] = s[h]


def _delta(qkv, z, zt, prow, pcol, reverse):
    ntok = z.shape[0]
    nblk = ntok // SCAN_ROWS
    cst = _delta_consts(reverse)
    names = ("tri", "trit", "incl", "strict", "eye", "m16", "q32", "q64", "bd")
    consts = [cst[n] for n in names]
    rb = functools.partial(_scan_block_index, nblk=nblk, reverse=reverse)
    full = lambda a: pl.BlockSpec(a.shape, lambda i: (0,) * a.ndim)
    return pl.pallas_call(
        functools.partial(_delta_kernel, reverse=reverse, dirn=1 if reverse else 0),
        out_shape=jax.ShapeDtypeStruct((ntok, DELTA_W), F32),
        grid=(nblk,),
        in_specs=[
            pl.BlockSpec((SCAN_ROWS, DELTA_W), lambda i: (rb(i), 0)),
            pl.BlockSpec((SCAN_ROWS, DELTA_W), lambda i: (rb(i), 1)),
            pl.BlockSpec((SCAN_ROWS, DELTA_W), lambda i: (rb(i), 2)),
            pl.BlockSpec((SCAN_ROWS, SM_W), lambda i: (rb(i), COL_SM // SM_W)),
            pl.BlockSpec((HEAD, SCAN_ROWS), lambda i: (0, rb(i))),
            full(prow), full(pcol)] + [full(a) for a in consts],
        out_specs=pl.BlockSpec((SCAN_ROWS, DELTA_W), lambda i: (rb(i), 0)),
        scratch_shapes=[pltpu.VMEM((DELTA_HEADS, HEAD, HEAD), F32),
                        pltpu.VMEM((SCAN_ROWS, HEAD), F32),
                        pltpu.VMEM((SCAN_ROWS, HEAD), F32),
                        pltpu.VMEM((SCAN_ROWS // CHUNK, 2 * DELTA_HEADS, CHUNK), F32)],
        compiler_params=_cparams(("arbitrary",)),
        name="delta_bwd" if reverse else "delta_fwd",
    )(qkv, qkv, qkv, z, zt, prow, pcol, *consts)


def _head_rms(x, w_ref, n_heads):
    parts = []
    for h in range(n_heads):
        xh = x[:, h * HEAD:(h + 1) * HEAD]
        parts.append(xh * lax.rsqrt(jnp.mean(xh * xh, axis=-1, keepdims=True) + RMS_EPS) * w_ref[...])
    return jnp.concatenate(parts, axis=1)


def _up_kernel(oaf_ref, oab_ref, od_ref, oef_ref, oeb_ref, gg_ref, eg_ref, ga_ref, gd_ref, ge_ref,
               nwa_ref, nwd_ref, nwe_ref, wa_ref, wd_ref, we_ref, y_ref, a_scr, d_scr, e_scr, *, lam_init):
    @pl.when(pl.program_id(1) == 0)
    def _():
        a_scr[...] = _bf(_head_rms(oaf_ref[...] + oab_ref[...], nwa_ref, GLA_HEADS) * _silu(gg_ref[...]))
        d_scr[...] = _bf(_head_rms(od_ref[...], nwd_ref, DIFF_HEADS) * (1.0 - lam_init))
        e_scr[...] = _bf(_head_rms(oef_ref[...] + oeb_ref[...], nwe_ref, DELTA_HEADS) * _silu(eg_ref[...]))

    y = _sigmoid(ga_ref[...]) * _dot(a_scr[...], wa_ref[...])
    y = y + _sigmoid(gd_ref[...]) * _dot(d_scr[...], wd_ref[...])
    y = y + _sigmoid(ge_ref[...]) * _dot(e_scr[...], we_ref[...])
    y_ref[...] = _bf(y)


def _up(oaf, oab, od, oef, oeb, z, nwa, nwd, nwe, wa, wd, we, lam_init, tm):
    ntok = z.shape[0]
    tn = UP_TN
    rows = lambda w: pl.BlockSpec((tm, w), lambda i, j: (i, 0))
    zcol = lambda col, w: pl.BlockSpec((tm, w), lambda i, j: (i, col // w))
    gate = lambda b: pl.BlockSpec((tm, tn), lambda i, j: (i, (COL_MG + b * D_MODEL) // tn + j))
    vec = pl.BlockSpec((1, HEAD), lambda i, j: (0, 0))
    wspec = lambda k: pl.BlockSpec((k, tn), lambda i, j: (0, j))
    return pl.pallas_call(
        functools.partial(_up_kernel, lam_init=lam_init),
        out_shape=jax.ShapeDtypeStruct((ntok, D_MODEL), BF16),
        grid=(ntok // tm, D_MODEL // tn),
        in_specs=[rows(GLA_W), rows(GLA_W), rows(DIFF_W), rows(DELTA_W), rows(DELTA_W),
                  zcol(COL_GG, GLA_W), zcol(COL_EG, DELTA_W), gate(0), gate(1), gate(2),
                  vec, vec, vec, wspec(GLA_W), wspec(DIFF_W), wspec(DELTA_W)],
        out_specs=pl.BlockSpec((tm, tn), lambda i, j: (i, j)),
        scratch_shapes=[pltpu.VMEM((tm, GLA_W), BF16), pltpu.VMEM((tm, DIFF_W), BF16), pltpu.VMEM((tm, DELTA_W), BF16)],
        compiler_params=_cparams(("arbitrary", "arbitrary")),
        name="up_merge",
    )(oaf, oab, od, oef, oeb, z, z, z, z, z, nwa.reshape(1, HEAD), nwd.reshape(1, HEAD), nwe.reshape(1, HEAD),
      wa, wd, we)


def _oproj_kernel(y_ref, x_ref, wo_ref, nw_ref, gate_ref, o_ref):
    tm = x_ref.shape[0]
    is_ctx = _row_is_ctx(pl.program_id(0) * tm, tm)
    t = _dot(y_ref[...], wo_ref[...])
    o_ref[...] = x_ref[...] + _pick_mod(is_ctx, gate_ref) * _rms(t, nw_ref[...])


def _oproj(y, xc, wo, norm_w, mods, tm):
    ntok, d = xc.shape
    return pl.pallas_call(
        _oproj_kernel,
        out_shape=jax.ShapeDtypeStruct((ntok, d), F32),
        grid=(ntok // tm,),
        in_specs=[
            pl.BlockSpec((tm, d), lambda i: (i, 0)),
            pl.BlockSpec((tm, d), lambda i: (i, 0)),
            pl.BlockSpec((d, d), lambda i: (0, 0)),
            pl.BlockSpec((1, d), lambda i: (0, 0)),
            pl.BlockSpec((8, d), lambda i: (0, 2)),
        ],
        out_specs=pl.BlockSpec((tm, d), lambda i: (i, 0)),
        compiler_params=_cparams(("arbitrary",)),
        name="out_proj",
    )(y, xc, wo, norm_w.reshape(1, d), mods)


def _ffn_kernel(x_ref, nw_ref, sh_ref, sc_ref, gate_ref, pw_ref, w1_ref, w3_ref, w2_ref, o_ref, h_scr, acc_scr):
    j = pl.program_id(1)
    tm = x_ref.shape[0]
    is_ctx = _row_is_ctx(pl.program_id(0) * tm, tm)

    @pl.when(j == 0)
    def _():
        h = _rms(x_ref[...], nw_ref[...])
        h = h * (1.0 + _pick_mod(is_ctx, sc_ref)) + _pick_mod(is_ctx, sh_ref)
        h_scr[...] = _bf(h)
        acc_scr[...] = jnp.zeros_like(acc_scr)

    h = h_scr[...]
    u = _silu(_dot(h, w1_ref[...])) * _dot(h, w3_ref[...])
    acc_scr[...] += _dot(_bf(u), w2_ref[...])

    @pl.when(j == pl.num_programs(1) - 1)
    def _():
        o_ref[...] = x_ref[...] + _pick_mod(is_ctx, gate_ref) * _rms(acc_scr[...], pw_ref[...])


def _ffn(xc, pre_w, post_w, mods, w1, w3, w2, tm):
    ntok, d = xc.shape
    dff = w1.shape[1]
    return pl.pallas_call(
        _ffn_kernel,
        out_shape=jax.ShapeDtypeStruct((ntok, d), F32),
        grid=(ntok // tm, dff // FF_TN),
        in_specs=[
            pl.BlockSpec((tm, d), lambda i, j: (i, 0), pipeline_mode=pl.Buffered(1)),
            pl.BlockSpec((1, d), lambda i, j: (0, 0)),
            pl.BlockSpec((8, d), lambda i, j: (0, 3)),
            pl.BlockSpec((8, d), lambda i, j: (0, 4)),
            pl.BlockSpec((8, d), lambda i, j: (0, 5)),
            pl.BlockSpec((1, d), lambda i, j: (0, 0)),
            pl.BlockSpec((d, FF_TN), lambda i, j: (0, j)),
            pl.BlockSpec((d, FF_TN), lambda i, j: (0, j)),
            pl.BlockSpec((FF_TN, d), lambda i, j: (j, 0)),
        ],
        out_specs=pl.BlockSpec((tm, d), lambda i, j: (i, 0), pipeline_mode=pl.Buffered(1)),
        scratch_shapes=[pltpu.VMEM((tm, d), BF16), pltpu.VMEM((tm, d), F32)],
        compiler_params=_cparams(("arbitrary", "arbitrary")),
        name="ffn",
    )(xc, pre_w.reshape(1, d), mods, mods, mods, post_w.reshape(1, d), w1, w3, w2)


def _permute_w_in(w):
    d = w.shape[0]
    pieces = [w[:, 0:1536], w[:, 1568:2080], w[:, 6720:7744], w[:, 3616:6688], w[:, 2080:3616], w[:, 7744:13888],
              w[:, 1536:1568], w[:, 6688:6720], jnp.zeros((d, SM_W - 64), w.dtype)]
    return _bf(jnp.concatenate(pieces, axis=1))


def _rope_tables(seq):
    rows = seq // GRID_W
    row_ids = jnp.repeat(jnp.arange(rows, dtype=F32), GRID_W)
    col_ids = jnp.tile(jnp.arange(GRID_W, dtype=F32), rows)
    half = DIFF_DQK // 2
    inv = 1.0 / (ROPE_BASE ** (jnp.arange(0, half, 2, dtype=F32) / half))
    ang_r = row_ids[:, None] * inv
    ang_c = col_ids[:, None] * inv
    cr, sr, cc, sc = jnp.cos(ang_r), jnp.sin(ang_r), jnp.cos(ang_c), jnp.sin(ang_c)
    cos_lat = jnp.tile(jnp.concatenate([cr, cr, cc, cc], axis=1), (1, 2))
    sin_lat = jnp.tile(jnp.concatenate([-sr, sr, -sc, sc], axis=1), (1, 2))
    cos_t = jnp.concatenate([jnp.ones((CTX_LEN, HEAD), F32), cos_lat], axis=0)
    sin_t = jnp.concatenate([jnp.zeros((CTX_LEN, HEAD), F32), sin_lat], axis=0)
    return cos_t, sin_t


def _lane_params(a_log, dt_bias):
    flat_a = a_log.reshape(-1)
    flat_b = dt_bias.reshape(-1)
    n = flat_a.shape[0]
    pa = jnp.zeros((HEAD,), F32).at[SM_A:SM_A + n].set(flat_a)
    pb = jnp.zeros((HEAD,), F32).at[SM_A:SM_A + n].set(flat_b)
    prow = jnp.zeros((8, HEAD), F32).at[0].set(pa).at[1].set(pb)
    pcol = jnp.zeros((HEAD, HEAD), F32).at[:, 0].set(pa).at[:, 1].set(pb)
    return prow, pcol


def _gla_gate_weights(w2, bias, dirn):
    wpad = jnp.zeros((HEAD, GLA_W), F32).at[GLA_GATE_RANK * dirn:GLA_GATE_RANK * (dirn + 1)].set(w2[dirn])
    return _bf(wpad), bias[dirn].reshape(1, GLA_W)


def kernel(x, c, ctx, c_ctx, ada_w, ada_b, mix_pre_w, mix_post_w, ffn_pre_w, ffn_post_w, w_in, gla_gate_w2, gla_gate_b, gla_norm_w, diff_lambda, diff_norm_w, delta_conv_w, delta_a_log, delta_dt_bias, delta_norm_w, w_up_gla, w_up_diff, w_up_delta, w_o, ffn_w1, ffn_w3, ffn_w2):
    seq = x.shape[1]
    ntok = CTX_LEN + seq
    tm = ROW_TILE if ntok % ROW_TILE == 0 else SCAN_ROWS
    xc = jnp.concatenate([ctx[0], x[0]], axis=0)
    cond = jnp.concatenate([c, c_ctx[None, :], jnp.zeros((COND_ROWS - 2, D_MODEL), F32)], axis=0)
    mods_all = _ada(_bf(jax.nn.silu(cond)), ada_w, ada_b)
    cos_t, sin_t = _rope_tables(seq)
    kv_tile = ROW_TILE if ntok % ROW_TILE == 0 else SCAN_ROWS

    for layer in range(DEPTH):
        lam_init = 0.8 - 0.6 * math.exp(-0.3 * layer)
        mods = mods_all[layer]
        z = _inproj(xc, mix_pre_w[layer], mods, _permute_w_in(w_in[layer]), tm)

        oa = []
        for dirn in range(2):
            w2pad, bias = _gla_gate_weights(gla_gate_w2[layer], gla_gate_b[layer], dirn)
            oa.append(_gla(z, w2pad, bias, reverse=bool(dirn)))

        qr, kr, vr = _rope(z, cos_t, sin_t)
        od_ctx = _attn(diff_lambda[layer], qr, kr, vr, 0, CTX_LEN, CTX_LEN, CTX_LEN, lam_init)
        od_lat = _attn(diff_lambda[layer], qr, kr, vr, CTX_LEN, seq, ntok, kv_tile, lam_init)
        od = jnp.concatenate([od_ctx, od_lat], axis=0)

        qkv = _conv(z, delta_conv_w[layer])
        zt = z[:, COL_SM:COL_SM + HEAD].T
        prow, pcol = _lane_params(delta_a_log[layer], delta_dt_bias[layer])
        oe = [_delta(qkv, z, zt, prow, pcol, reverse=bool(dirn)) for dirn in range(2)]

        y = _up(oa[0], oa[1], od, oe[0], oe[1], z, gla_norm_w[layer], diff_norm_w[layer], delta_norm_w[layer],
                _bf(w_up_gla[layer]), _bf(w_up_diff[layer]), _bf(w_up_delta[layer]), lam_init, tm)
        xc = _oproj(y, xc, _bf(w_o[layer]), mix_post_w[layer], mods, tm)
        xc = _ffn(xc, ffn_pre_w[layer], ffn_post_w[layer], mods, _bf(ffn_w1[layer]), _bf(ffn_w3[layer]),
                  _bf(ffn_w2[layer]), tm)
    return xc[CTX_LEN:][None]
```

```python
import functools
import math

import numpy as np
import jax
import jax.numpy as jnp
from jax import lax
from jax.experimental import pallas as pl
from jax.experimental.pallas import tpu as pltpu

F32 = jnp.float32
BF16 = jnp.bfloat16

D_MODEL = 2048
DEPTH = 2
GRID_W = 64
CTX_LEN = 256
HEAD = 128
GLA_HEADS = 4
GLA_GATE_RANK = 16
GLA_GATE_NORM = 16.0
DIFF_HEADS = 4
DIFF_DQK = 64
DELTA_HEADS = 8
DELTA_CONV = 5
CHUNK = 64
ROPE_BASE = 10000.0
RMS_EPS = 1e-6
L2_EPS = 1e-6
D_FF = 5632
GLA_W = GLA_HEADS * HEAD
DIFF_W = DIFF_HEADS * HEAD
DELTA_W = DELTA_HEADS * HEAD

COL_GQ, COL_GK, COL_GV, COL_GG = 0, 512, 1024, 1536
COL_EG = 2048
COL_EQ, COL_EK, COL_EV = 3072, 4096, 5120
COL_DQ, COL_DK, COL_DV = 6144, 6656, 7168
COL_MG = 7680
NZ = 13824
SM_W = 256
SM_A = 32
SM_B = 48

ROW_TILE = 768
IN_TM = 1056
SCAN_ROWS = 256
IN_TN = 1536
FF_TN = 512
UP_TN = 512
ATT_TQ = 256
COND_ROWS = 16
CONV_HALO = 16
VMEM_LIMIT = 56 * 1024 * 1024


def _cparams(sem, vmem=VMEM_LIMIT):
    return pltpu.CompilerParams(dimension_semantics=sem, vmem_limit_bytes=vmem)


def _bf(x):
    return x.astype(BF16)


def _dot(a, b):
    return jnp.dot(a, b, preferred_element_type=F32)


def _dot_nt(a, b):
    return lax.dot_general(a, b, (((1,), (1,)), ((), ())), preferred_element_type=F32)


def _dot_tn(a, b):
    return lax.dot_general(a, b, (((0,), (0,)), ((), ())), preferred_element_type=F32)


def _split3(x):
    hi = _bf(x)
    r = x - hi.astype(F32)
    mid = _bf(r)
    lo = _bf(r - mid.astype(F32))
    return hi, mid, lo


def _dot_exact_lhs(m_bf, x):
    hi, mid, lo = _split3(x)
    return _dot(m_bf, hi) + _dot(m_bf, mid) + _dot(m_bf, lo)


def _dot_exact_rhs(x, m_bf):
    hi, mid, lo = _split3(x)
    return _dot(hi, m_bf) + _dot(mid, m_bf) + _dot(lo, m_bf)


def _dot3(a, b):
    ah = _bf(a)
    al = _bf(a - ah.astype(F32))
    bh = _bf(b)
    bl = _bf(b - bh.astype(F32))
    return _dot(ah, bh) + _dot(ah, bl) + _dot(al, bh)


def _sigmoid(x):
    return 1.0 / (1.0 + jnp.exp(-x))


def _silu(x):
    return x * _sigmoid(x)


def _softplus(x):
    return jnp.maximum(x, 0.0) + jnp.log1p(jnp.exp(-jnp.abs(x)))


def _log_sigmoid(x):
    return jnp.minimum(x, 0.0) - jnp.log1p(jnp.exp(-jnp.abs(x)))


def _rms(x, w):
    return x * lax.rsqrt(jnp.mean(x * x, axis=-1, keepdims=True) + RMS_EPS) * w


ROW_CHUNK = 16
ROW_UNROLL = 6


def _mod_row(mod_ref, row0):
    return mod_ref[pl.ds((row0 < CTX_LEN).astype(jnp.int32), 1), :]


def _norm_modulate(x_ref, nw_ref, sh_ref, sc_ref, h_ref, row0):
    def body(r, carry):
        rows = pl.ds(pl.multiple_of(r * ROW_CHUNK, ROW_CHUNK), ROW_CHUNK)
        g0 = row0 + r * ROW_CHUNK
        h = _rms(x_ref[rows, :], nw_ref[...])
        h_ref[rows, :] = _bf(h * (1.0 + _mod_row(sc_ref, g0)) + _mod_row(sh_ref, g0))
        return carry

    lax.fori_loop(0, x_ref.shape[0] // ROW_CHUNK, body, 0, unroll=ROW_UNROLL)


def _gated_residual(x_ref, t_ref, nw_ref, gate_ref, o_ref, row0):
    def body(r, carry):
        rows = pl.ds(pl.multiple_of(r * ROW_CHUNK, ROW_CHUNK), ROW_CHUNK)
        g0 = row0 + r * ROW_CHUNK
        o_ref[rows, :] = x_ref[rows, :] + _mod_row(gate_ref, g0) * _rms(t_ref[rows, :], nw_ref[...])
        return carry

    lax.fori_loop(0, x_ref.shape[0] // ROW_CHUNK, body, 0, unroll=ROW_UNROLL)


def _ada_kernel(s_ref, w_ref, b_ref, o_ref):
    o_ref[0] = _dot(s_ref[...], _bf(w_ref[0])) + b_ref[0]


def _ada(s_bf, ada_w, ada_b):
    depth, d, n6 = ada_w.shape
    tn = 1024
    return pl.pallas_call(
        _ada_kernel,
        out_shape=jax.ShapeDtypeStruct((depth, COND_ROWS, n6), F32),
        grid=(depth, n6 // tn),
        in_specs=[
            pl.BlockSpec((COND_ROWS, d), lambda l, j: (0, 0)),
            pl.BlockSpec((1, d, tn), lambda l, j: (l, 0, j)),
            pl.BlockSpec((1, 1, tn), lambda l, j: (l, 0, j)),
        ],
        out_specs=pl.BlockSpec((1, COND_ROWS, tn), lambda l, j: (l, 0, j)),
        compiler_params=_cparams(("arbitrary", "arbitrary")),
        name="ada",
    )(s_bf, ada_w, ada_b.reshape(depth, 1, n6))


def _inproj_kernel(x_ref, nw_ref, sh_ref, sc_ref, w_ref, ws_ref, o_ref, os_ref, h_scr):
    @pl.when(pl.program_id(1) == 0)
    def _():
        _norm_modulate(x_ref, nw_ref, sh_ref, sc_ref, h_scr, pl.program_id(0) * x_ref.shape[0])
        os_ref[...] = _dot(h_scr[...], ws_ref[...])

    o_ref[...] = _bf(_dot(h_scr[...], w_ref[...]))


def _inproj(xc, norm_w, mods, w_bf, ws_bf, layer, tm):
    ntok, d = xc.shape
    nz = w_bf.shape[2]
    return pl.pallas_call(
        _inproj_kernel,
        out_shape=(jax.ShapeDtypeStruct((ntok, nz), BF16), jax.ShapeDtypeStruct((ntok, SM_W), F32)),
        grid=(ntok // tm, nz // IN_TN),
        in_specs=[
            pl.BlockSpec((tm, d), lambda i, j: (i, 0), pipeline_mode=pl.Buffered(1)),
            pl.BlockSpec((1, d), lambda i, j: (0, 0)),
            pl.BlockSpec((8, d), lambda i, j: (0, 0)),
            pl.BlockSpec((8, d), lambda i, j: (0, 1)),
            pl.BlockSpec((None, d, IN_TN), lambda i, j: (layer, 0, j)),
            pl.BlockSpec((None, d, SM_W), lambda i, j: (layer, 0, 0)),
        ],
        out_specs=(pl.BlockSpec((tm, IN_TN), lambda i, j: (i, j)), pl.BlockSpec((tm, SM_W), lambda i, j: (i, 0))),
        scratch_shapes=[pltpu.VMEM((tm, d), BF16)],
        compiler_params=_cparams(("arbitrary", "arbitrary")),
        name="inproj",
    )(xc, norm_w.reshape(1, d), mods, mods, w_bf, ws_bf)


def _gla_consts(reverse):
    c = CHUNK
    i = np.arange(c)[:, None]
    t = np.arange(c)[None, :]
    if not reverse:
        mats = [t <= i, t > i]
    else:
        mats = [t >= i, t < i]
    masks = [i == t]
    for m in (32, 16, 8, 4, 2, 1):
        p = (i // (2 * m)) * (2 * m) + m - 1
        second = ((i // m) % 2) == 1
        same = (i // (2 * m)) == (t // (2 * m))
        t_second = ((t // m) % 2) == 1
        if not reverse:
            w = np.where(second, (t > p) & (t <= i), (t > i) & (t <= p))
            pm = same & second & ~t_second
        else:
            w = np.where(second, (t >= p + 1) & (t <= i - 1), (t >= i) & (t <= p))
            pm = same & ~second & t_second
        mats.append(w)
        masks.append(pm)
    wall = np.concatenate([m_.astype(np.float32) for m_ in mats], axis=0)
    pmask = np.stack([m_.astype(np.float32) for m_ in masks], axis=0)
    return jnp.asarray(wall, BF16), jnp.asarray(pmask, F32)


def _gla_kernel(q_ref, k_ref, v_ref, sm_ref, w2_ref, b_ref, wall_ref, pm_ref, o_ref, st_scr, g_scr,
                *, reverse):
    @pl.when(pl.program_id(0) == 0)
    def _():
        st_scr[...] = jnp.zeros_like(st_scr)

    x = _dot(_bf(sm_ref[:, 0:HEAD]), w2_ref[...]) + b_ref[...]
    g_scr[...] = _log_sigmoid(x) * (1.0 / GLA_GATE_NORM)
    n_chunks = q_ref.shape[0] // CHUNK
    last = 0 if reverse else CHUNK - 1
    scale = HEAD ** -0.5

    units = []
    for cc in range(n_chunks):
        rows = slice(cc * CHUNK, (cc + 1) * CHUNK)
        e_all = jnp.exp(_dot_exact_lhs(wall_ref[...], g_scr[rows, :]))
        for h in range(GLA_HEADS):
            cols = slice(h * HEAD, (h + 1) * HEAD)
            units.append((cc, h, q_ref[rows, cols].astype(F32) * scale, k_ref[rows, cols].astype(F32),
                          v_ref[rows, cols], e_all[:, cols]))
    att = [jnp.where(pm_ref[0] > 0.0, _dot_nt(_bf(q), _bf(k)), 0.0) for (_, _, q, k, _, _) in units]
    for lv in range(6):
        lvl = slice((2 + lv) * CHUNK, (3 + lv) * CHUNK)
        pm = pm_ref[lv + 1] > 0.0
        att = [a + jnp.where(pm, _dot_nt(_bf(q * e[lvl]), _bf(k * e[lvl])), 0.0)
               for a, (_, _, q, k, _, e) in zip(att, units)]
    o_intra = {(cc, h): _dot(_bf(a), vb) for a, (cc, h, _, _, vb, _) in zip(att, units)}
    q_in = {(cc, h): _bf(q * e[0:CHUNK]) for (cc, h, q, _, _, e) in units}
    k_st = {(cc, h): _bf(k * e[CHUNK:2 * CHUNK]) for (cc, h, _, k, _, e) in units}
    v_bf = {(cc, h): vb for (cc, h, _, _, vb, _) in units}
    e_last = {(cc, h): e[last:last + 1, :] for (cc, h, _, _, _, e) in units}

    st = [st_scr[h] for h in range(GLA_HEADS)]
    for ci in range(n_chunks):
        cc = (n_chunks - 1 - ci) if reverse else ci
        for h in range(GLA_HEADS):
            o_ref[cc * CHUNK:(cc + 1) * CHUNK, h * HEAD:(h + 1) * HEAD] = (
                _dot_nt(q_in[(cc, h)], _bf(st[h])) + o_intra[(cc, h)])
            st[h] = st[h] * e_last[(cc, h)] + _dot_tn(v_bf[(cc, h)], k_st[(cc, h)])
    for h in range(GLA_HEADS):
        st_scr[h] = st[h]


def _scan_block_index(i, nblk, reverse):
    if not reverse:
        return i
    nctx = CTX_LEN // SCAN_ROWS
    return jnp.where(i < nctx, nctx - 1 - i, nblk - 1 + nctx - i)


def _gla(z, zs, w2pad, bias, reverse):
    ntok = z.shape[0]
    nblk = ntok // SCAN_ROWS
    wall, pmask = _gla_consts(reverse)
    rb = functools.partial(_scan_block_index, nblk=nblk, reverse=reverse)
    return pl.pallas_call(
        functools.partial(_gla_kernel, reverse=reverse),
        out_shape=jax.ShapeDtypeStruct((ntok, GLA_W), F32),
        grid=(nblk,),
        in_specs=[
            pl.BlockSpec((SCAN_ROWS, GLA_W), lambda i: (rb(i), COL_GQ // GLA_W)),
            pl.BlockSpec((SCAN_ROWS, GLA_W), lambda i: (rb(i), COL_GK // GLA_W)),
            pl.BlockSpec((SCAN_ROWS, GLA_W), lambda i: (rb(i), COL_GV // GLA_W)),
            pl.BlockSpec((SCAN_ROWS, SM_W), lambda i: (rb(i), 0)),
            pl.BlockSpec((HEAD, GLA_W), lambda i: (0, 0)),
            pl.BlockSpec((1, GLA_W), lambda i: (0, 0)),
            pl.BlockSpec(wall.shape, lambda i: (0, 0)),
            pl.BlockSpec(pmask.shape, lambda i: (0, 0, 0)),
        ],
        out_specs=pl.BlockSpec((SCAN_ROWS, GLA_W), lambda i: (rb(i), 0)),
        scratch_shapes=[pltpu.VMEM((GLA_HEADS, HEAD, HEAD), F32), pltpu.VMEM((SCAN_ROWS, GLA_W), F32)],
        compiler_params=_cparams(("arbitrary",)),
        name="gla_bwd" if reverse else "gla_fwd",
    )(z, z, z, zs, w2pad, bias, wall, pmask)


def _rope_kernel(q_ref, k_ref, v_ref, cos_ref, sin_ref, qo_ref, ko_ref, vo_ref):
    w = q_ref.shape[1]
    cos = jnp.concatenate([cos_ref[...]] * DIFF_HEADS, axis=1)
    sin = jnp.concatenate([sin_ref[...]] * DIFF_HEADS, axis=1)
    lane = lax.broadcasted_iota(jnp.int32, q_ref.shape, 1)
    first = (lane % 32) < 16

    def rot(x):
        swapped = jnp.where(first, pltpu.roll(x, w - 16, axis=1), pltpu.roll(x, 16, axis=1))
        return x * cos + swapped * sin

    qo_ref[...] = _bf(rot(q_ref[...].astype(F32)) * (DIFF_DQK ** -0.5 * math.log2(math.e)))
    ko_ref[...] = _bf(rot(k_ref[...].astype(F32)))
    lane_h = lax.broadcasted_iota(jnp.int32, (q_ref.shape[0], HEAD), 1)
    ones_col = jnp.where(lane_h == 0, 1.0, 0.0).astype(BF16)
    for h in range(DIFF_HEADS):
        vo_ref[:, 2 * h * HEAD:(2 * h + 1) * HEAD] = v_ref[:, h * HEAD:(h + 1) * HEAD]
        vo_ref[:, (2 * h + 1) * HEAD:(2 * h + 2) * HEAD] = ones_col


def _rope(z, cos_t, sin_t):
    ntok = z.shape[0]
    tm = SCAN_ROWS
    spec = lambda col: pl.BlockSpec((tm, DIFF_W), lambda i: (i, col // DIFF_W))
    out = jax.ShapeDtypeStruct((ntok, DIFF_W), BF16)
    return pl.pallas_call(
        _rope_kernel,
        out_shape=(out, out, jax.ShapeDtypeStruct((ntok, 2 * DIFF_W), BF16)),
        grid=(ntok // tm,),
        in_specs=[spec(COL_DQ), spec(COL_DK), spec(COL_DV),
                  pl.BlockSpec((tm, HEAD), lambda i: (i, 0)),
                  pl.BlockSpec((tm, HEAD), lambda i: (i, 0))],
        out_specs=(pl.BlockSpec((tm, DIFF_W), lambda i: (i, 0)), pl.BlockSpec((tm, DIFF_W), lambda i: (i, 0)),
                   pl.BlockSpec((tm, 2 * DIFF_W), lambda i: (i, 0))),
        compiler_params=_cparams(("arbitrary",)),
        name="rope",
    )(z, z, z, cos_t, sin_t)


def _attn_kernel(lam_ref, q_ref, k_ref, v_ref, o_ref, *, kv_tiles_ctx, kv_tiles_all, lam_init):
    tq = q_ref.shape[0]
    q = q_ref[...]
    lane = lax.broadcasted_iota(jnp.int32, q.shape, 1)
    zero = jnp.zeros_like(q)
    q2 = jnp.concatenate([jnp.where(lane < DIFF_DQK, q, zero), jnp.where(lane >= DIFF_DQK, q, zero)], axis=0)
    lp = lam_ref[...]
    lam = (jnp.exp(jnp.sum(lp[0:1] * lp[1:2], axis=-1, keepdims=True))
           - jnp.exp(jnp.sum(lp[2:3] * lp[3:4], axis=-1, keepdims=True)) + lam_init)

    def attend(kv_tiles):
        m = jnp.full((2 * tq, 1), -jnp.inf, F32)
        acc = jnp.zeros((2 * tq, 2 * HEAD), F32)
        for r0, r1 in kv_tiles:
            s = _dot_nt(q2, k_ref[r0:r1, :])
            m_new = jnp.maximum(m, jnp.max(s, axis=-1, keepdims=True))
            acc = jnp.exp2(m - m_new) * acc + _dot(_bf(jnp.exp2(s - m_new)), v_ref[r0:r1, :])
            m = m_new
        o = acc[:, 0:HEAD] / acc[:, HEAD:HEAD + 1]
        o_ref[...] = o[0:tq] - lam * o[tq:2 * tq]

    is_ctx = pl.program_id(1) < CTX_LEN // tq

    @pl.when(is_ctx)
    def _():
        attend(kv_tiles_ctx)

    @pl.when(jnp.logical_not(is_ctx))
    def _():
        attend(kv_tiles_all)


def _kv_tiles(n, tile):
    return tuple((r, min(r + tile, n)) for r in range(0, n, tile))


def _attn(lam_p, q, k, v, kv_tile, lam_init):
    ntok = q.shape[0]
    tq = ATT_TQ
    return pl.pallas_call(
        functools.partial(_attn_kernel, kv_tiles_ctx=_kv_tiles(CTX_LEN, kv_tile), kv_tiles_all=_kv_tiles(ntok, kv_tile),
                          lam_init=lam_init),
        out_shape=jax.ShapeDtypeStruct((ntok, DIFF_W), F32),
        grid=(DIFF_HEADS, ntok // tq),
        in_specs=[
            pl.BlockSpec(lam_p.shape, lambda h, i: (0, 0)),
            pl.BlockSpec((tq, HEAD), lambda h, i: (i, h)),
            pl.BlockSpec((ntok, HEAD), lambda h, i: (0, h)),
            pl.BlockSpec((ntok, 2 * HEAD), lambda h, i: (0, h)),
        ],
        out_specs=pl.BlockSpec((tq, HEAD), lambda h, i: (i, h)),
        compiler_params=_cparams(("arbitrary", "arbitrary")),
        name="diff_attn",
    )(lam_p, q, k, v)


def _conv_kernel(xp_ref, x_ref, xn_ref, w_ref, o_ref):
    i = pl.program_id(0)
    nblk = pl.num_programs(0)
    nctx = CTX_LEN // SCAN_ROWS
    tm = x_ref.shape[0]
    pad = DELTA_CONV // 2
    has_prev = jnp.logical_and(i != 0, i != nctx)
    has_next = jnp.logical_and(i != nctx - 1, i != nblk - 1)
    prev = jnp.where(has_prev, xp_ref[...].astype(F32), 0.0)
    nxt = jnp.where(has_next, xn_ref[...].astype(F32), 0.0)
    xe = jnp.concatenate([prev, x_ref[...].astype(F32), nxt], axis=0)
    acc = None
    for j in range(DELTA_CONV):
        off = CONV_HALO + j - pad
        term = xe[off:off + tm, :] * w_ref[j:j + 1, :]
        acc = term if acc is None else acc + term
    y = _silu(acc)
    jc = pl.program_id(1)
    q_blocks = DELTA_W // x_ref.shape[1]
    is_qk = jc < 2 * q_blocks
    post = jnp.where(jc < q_blocks, HEAD ** -0.5, 1.0)
    for hh in range(x_ref.shape[1] // HEAD):
        cols = slice(hh * HEAD, (hh + 1) * HEAD)
        yh = y[:, cols]
        yn = yh * lax.rsqrt(jnp.sum(yh * yh, axis=-1, keepdims=True) + L2_EPS) * post
        o_ref[:, cols] = jnp.where(is_qk, yn, yh)


def _conv(z, conv_w):
    ntok = z.shape[0]
    tm = SCAN_ROWS
    width = 3 * DELTA_W
    nbh = ntok // CONV_HALO
    rh = tm // CONV_HALO
    cw = DELTA_W
    ncb = width // cw
    c0 = COL_EQ // cw
    return pl.pallas_call(
        _conv_kernel,
        out_shape=jax.ShapeDtypeStruct((ntok, width), F32),
        grid=(ntok // tm, ncb),
        in_specs=[
            pl.BlockSpec((CONV_HALO, cw), lambda i, j: (jnp.maximum(i * rh - 1, 0), c0 + j)),
            pl.BlockSpec((tm, cw), lambda i, j: (i, c0 + j)),
            pl.BlockSpec((CONV_HALO, cw), lambda i, j: (jnp.minimum((i + 1) * rh, nbh - 1), c0 + j)),
            pl.BlockSpec((DELTA_CONV, cw), lambda i, j: (0, j)),
        ],
        out_specs=pl.BlockSpec((tm, cw), lambda i, j: (i, j)),
        compiler_params=_cparams(("arbitrary", "arbitrary")),
        name="delta_conv",
    )(z, z, z, conv_w)


def _delta_consts(reverse):
    c = CHUNK
    i = np.arange(c)[:, None]
    t = np.arange(c)[None, :]
    tri = (t >= i) if reverse else (t <= i)
    incl = tri
    strict = (t > i) if reverse else (t < i)
    cc = np.arange(4 * c)[None, :]
    j = cc % c
    eye = (i == j)
    m16 = (i // 16) == (j // 16)
    q32 = ((i // 32) == (j // 32)) & ~m16
    q64 = (i // 32) != (j // 32)
    r = np.arange(4 * c)[:, None]
    bd = (r // c) == (cc // c)
    f = lambda a: jnp.asarray(a.astype(np.float32))
    return dict(tri=jnp.asarray(tri.astype(np.float32), BF16), trit=jnp.asarray(tri.T.astype(np.float32), BF16),
                incl=f(incl), strict=f(strict), eye=f(eye), m16=f(m16), q32=f(q32), q64=f(q64),
                bd=jnp.asarray(bd.astype(np.float32), BF16))


def _blockdiag(xcat, bd):
    return jnp.concatenate([xcat] * 4, axis=0) * bd


def _delta_kernel(q_ref, k_ref, v_ref, sm_ref, smt_ref, pr_ref, pc_ref, tri_ref, trit_ref, incl_ref, strict_ref,
                  eye_ref, m16_ref, q32_ref, q64_ref, bd_ref, o_ref, s_scr, g_scr, b_scr, gt_scr, *, reverse, dirn):
    @pl.when(pl.program_id(0) == 0)
    def _():
        s_scr[...] = jnp.zeros_like(s_scr)

    n_chunks = q_ref.shape[0] // CHUNK
    sm = sm_ref[:, 0:HEAD]
    g_scr[...] = -jnp.exp(pr_ref[0:1, :]) * _softplus(sm + pr_ref[1:2, :])
    b_scr[...] = _sigmoid(sm)
    gt = -jnp.exp(pc_ref[:, 0:1]) * _softplus(smt_ref[...] + pc_ref[:, 1:2])
    for c in range(n_chunks):
        gt_scr[c] = _dot_exact_rhs(gt[SM_A:SM_A + 2 * DELTA_HEADS, c * CHUNK:(c + 1) * CHUNK], trit_ref[...])
    last = 0 if reverse else CHUNK - 1
    incl = incl_ref[...] > 0.0
    strict = strict_ref[...] > 0.0
    bd = bd_ref[...]

    def mm(a, b):
        return _dot(_bf(a), _blockdiag(_bf(b), bd))

    local = {}
    units = []
    for cc in range(n_chunks):
        rows = slice(cc * CHUNK, (cc + 1) * CHUNK)
        gc = _dot_exact_lhs(tri_ref[...], g_scr[rows, :])
        beta = b_scr[rows, :]
        gct = gt_scr[cc]
        for grp in range(DELTA_HEADS // 4):
            a_list, att_list, rhs_list, misc = [], [], [], []
            for hh in range(4):
                h = grp * 4 + hh
                cols = slice(h * HEAD, (h + 1) * HEAD)
                la = SM_A + DELTA_HEADS * dirn + h
                lb = SM_B + DELTA_HEADS * dirn + h
                gcol = gc[:, la:la + 1]
                grow = gct[DELTA_HEADS * dirn + h:DELTA_HEADS * dirn + h + 1, :]
                bcol = beta[:, lb:lb + 1]
                q = q_ref[rows, cols]
                k = k_ref[rows, cols]
                v = v_ref[rows, cols]
                dec = jnp.exp(jnp.where(incl, gcol - grow, -jnp.inf))
                kb = k * bcol
                kbf = _bf(k)
                a_list.append(jnp.where(strict, _dot_nt(_bf(kb), kbf) * dec, 0.0))
                att_list.append(_bf(_dot_nt(_bf(q), kbf) * dec))
                eg = jnp.exp(gcol)
                rhs_list.append(_bf(jnp.concatenate([v * bcol, kb * eg], axis=1)))
                g_last = gcol[last:last + 1, :]
                misc.append((_bf(q * eg), _bf(k * jnp.exp(g_last - gcol)), jnp.exp(g_last)))
            units.append((cc, grp, jnp.concatenate(a_list, axis=1), att_list, jnp.concatenate(rhs_list, axis=0), misc))

    acat = [u_[2] for u_ in units]
    n = [-(a * m16_ref[...]) for a in acat]
    p = [eye_ref[...] + n_ for n_ in n]
    for _ in range(3):
        n = [mm(n_, n_) for n_ in n]
        p = [p_ + mm(p_, n_) for p_, n_ in zip(p, n)]
    for q_ref_ in (q32_ref, q64_ref):
        x = [mm(p_, a * q_ref_[...]) for p_, a in zip(p, acat)]
        p = [p_ - mm(x_, p_) for p_, x_ in zip(p, x)]
    sol = [_dot(_blockdiag(_bf(p_), bd), u_[4]) for p_, u_ in zip(p, units)]
    for (cc, grp, _, att_list, _, misc), sol_ in zip(units, sol):
        for hh in range(4):
            u = sol_[hh * CHUNK:(hh + 1) * CHUNK, 0:HEAD]
            w = _bf(sol_[hh * CHUNK:(hh + 1) * CHUNK, HEAD:2 * HEAD])
            local[(cc, grp * 4 + hh)] = (u, w, att_list[hh]) + misc[hh]

    s = [s_scr[h] for h in range(DELTA_HEADS)]
    for ci in range(n_chunks):
        cc = (n_chunks - 1 - ci) if reverse else ci
        sb = [_bf(s_) for s_ in s]
        vnb = [_bf(local[(cc, h)][0] - _dot(local[(cc, h)][1], sb[h])) for h in range(DELTA_HEADS)]
        for h in range(DELTA_HEADS):
            _, _, att, qd, kd, egl = local[(cc, h)]
            o_ref[cc * CHUNK:(cc + 1) * CHUNK, h * HEAD:(h + 1) * HEAD] = _dot(qd, sb[h]) + _dot(att, vnb[h])
            s[h] = s[h] * egl + _dot_tn(kd, vnb[h])
    for h in range(DELTA_HEADS):
        s_scr[h] = s[h]


def _delta(qkv, zs, zt, prow, pcol, reverse):
    ntok = zs.shape[0]
    nblk = ntok // SCAN_ROWS
    cst = _delta_consts(reverse)
    names = ("tri", "trit", "incl", "strict", "eye", "m16", "q32", "q64", "bd")
    consts = [cst[n] for n in names]
    rb = functools.partial(_scan_block_index, nblk=nblk, reverse=reverse)
    full = lambda a: pl.BlockSpec(a.shape, lambda i: (0,) * a.ndim)
    return pl.pallas_call(
        functools.partial(_delta_kernel, reverse=reverse, dirn=1 if reverse else 0),
        out_shape=jax.ShapeDtypeStruct((ntok, DELTA_W), F32),
        grid=(nblk,),
        in_specs=[
            pl.BlockSpec((SCAN_ROWS, DELTA_W), lambda i: (rb(i), 0)),
            pl.BlockSpec((SCAN_ROWS, DELTA_W), lambda i: (rb(i), 1)),
            pl.BlockSpec((SCAN_ROWS, DELTA_W), lambda i: (rb(i), 2)),
            pl.BlockSpec((SCAN_ROWS, SM_W), lambda i: (rb(i), 0)),
            pl.BlockSpec((HEAD, SCAN_ROWS), lambda i: (0, rb(i))),
            full(prow), full(pcol)] + [full(a) for a in consts],
        out_specs=pl.BlockSpec((SCAN_ROWS, DELTA_W), lambda i: (rb(i), 0)),
        scratch_shapes=[pltpu.VMEM((DELTA_HEADS, HEAD, HEAD), F32),
                        pltpu.VMEM((SCAN_ROWS, HEAD), F32),
                        pltpu.VMEM((SCAN_ROWS, HEAD), F32),
                        pltpu.VMEM((SCAN_ROWS // CHUNK, 2 * DELTA_HEADS, CHUNK), F32)],
        compiler_params=_cparams(("arbitrary",)),
        name="delta_bwd" if reverse else "delta_fwd",
    )(qkv, qkv, qkv, zs, zt, prow, pcol, *consts)


def _head_rms(x, w_ref, n_heads):
    parts = []
    for h in range(n_heads):
        xh = x[:, h * HEAD:(h + 1) * HEAD]
        parts.append(xh * lax.rsqrt(jnp.mean(xh * xh, axis=-1, keepdims=True) + RMS_EPS) * w_ref[...])
    return jnp.concatenate(parts, axis=1)


def _up_kernel(oaf_ref, oab_ref, od_ref, oef_ref, oeb_ref, gg_ref, eg_ref, ga_ref, gd_ref, ge_ref,
               nwa_ref, nwd_ref, nwe_ref, wa_ref, wd_ref, we_ref, y_ref, a_scr, d_scr, e_scr, *, lam_init):
    @pl.when(pl.program_id(1) == 0)
    def _():
        gg = gg_ref[...].astype(F32)
        eg = eg_ref[...].astype(F32)
        a_scr[...] = _bf(_head_rms(oaf_ref[...] + oab_ref[...], nwa_ref, GLA_HEADS) * _silu(gg))
        d_scr[...] = _bf(_head_rms(od_ref[...], nwd_ref, DIFF_HEADS) * (1.0 - lam_init))
        e_scr[...] = _bf(_head_rms(oef_ref[...] + oeb_ref[...], nwe_ref, DELTA_HEADS) * _silu(eg))

    y = _sigmoid(ga_ref[...].astype(F32)) * _dot(a_scr[...], wa_ref[...])
    y = y + _sigmoid(gd_ref[...].astype(F32)) * _dot(d_scr[...], wd_ref[...])
    y = y + _sigmoid(ge_ref[...].astype(F32)) * _dot(e_scr[...], we_ref[...])
    y_ref[...] = _bf(y)


def _up(oaf, oab, od, oef, oeb, z, nwa, nwd, nwe, wa, wd, we, layer, lam_init, tm):
    ntok = z.shape[0]
    tn = UP_TN
    rows = lambda w: pl.BlockSpec((tm, w), lambda i, j: (i, 0))
    zcol = lambda col, w: pl.BlockSpec((tm, w), lambda i, j: (i, col // w))
    gate = lambda b: pl.BlockSpec((tm, tn), lambda i, j: (i, (COL_MG + b * D_MODEL) // tn + j))
    vec = pl.BlockSpec((1, HEAD), lambda i, j: (0, 0))
    wspec = lambda k: pl.BlockSpec((None, k, tn), lambda i, j: (layer, 0, j))
    return pl.pallas_call(
        functools.partial(_up_kernel, lam_init=lam_init),
        out_shape=jax.ShapeDtypeStruct((ntok, D_MODEL), BF16),
        grid=(ntok // tm, D_MODEL // tn),
        in_specs=[rows(GLA_W), rows(GLA_W), rows(DIFF_W), rows(DELTA_W), rows(DELTA_W),
                  zcol(COL_GG, GLA_W), zcol(COL_EG, DELTA_W), gate(0), gate(1), gate(2),
                  vec, vec, vec, wspec(GLA_W), wspec(DIFF_W), wspec(DELTA_W)],
        out_specs=pl.BlockSpec((tm, tn), lambda i, j: (i, j)),
        scratch_shapes=[pltpu.VMEM((tm, GLA_W), BF16), pltpu.VMEM((tm, DIFF_W), BF16), pltpu.VMEM((tm, DELTA_W), BF16)],
        compiler_params=_cparams(("arbitrary", "arbitrary")),
        name="up_merge",
    )(oaf, oab, od, oef, oeb, z, z, z, z, z, nwa.reshape(1, HEAD), nwd.reshape(1, HEAD), nwe.reshape(1, HEAD),
      wa, wd, we)


def _oproj_kernel(y_ref, x_ref, wo_ref, nw_ref, gate_ref, o_ref, t_scr):
    t_scr[...] = _dot(y_ref[...], wo_ref[...])
    _gated_residual(x_ref, t_scr, nw_ref, gate_ref, o_ref, pl.program_id(0) * x_ref.shape[0])


def _oproj(y, xc, wo, norm_w, mods, layer, tm):
    ntok, d = xc.shape
    return pl.pallas_call(
        _oproj_kernel,
        out_shape=jax.ShapeDtypeStruct((ntok, d), F32),
        grid=(ntok // tm,),
        in_specs=[
            pl.BlockSpec((tm, d), lambda i: (i, 0)),
            pl.BlockSpec((tm, d), lambda i: (i, 0)),
            pl.BlockSpec((None, d, d), lambda i: (layer, 0, 0), pipeline_mode=pl.Buffered(1)),
            pl.BlockSpec((1, d), lambda i: (0, 0)),
            pl.BlockSpec((8, d), lambda i: (0, 2)),
        ],
        out_specs=pl.BlockSpec((tm, d), lambda i: (i, 0)),
        scratch_shapes=[pltpu.VMEM((tm, d), F32)],
        compiler_params=_cparams(("arbitrary",)),
        name="out_proj",
    )(y, xc, wo, norm_w.reshape(1, d), mods)


def _ffn_kernel(x_ref, nw_ref, sh_ref, sc_ref, gate_ref, pw_ref, w1_ref, w3_ref, w2_ref, o_ref, h_scr, acc_scr):
    j = pl.program_id(1)
    row0 = pl.program_id(0) * x_ref.shape[0]

    @pl.when(j == 0)
    def _():
        _norm_modulate(x_ref, nw_ref, sh_ref, sc_ref, h_scr, row0)
        acc_scr[...] = jnp.zeros_like(acc_scr)

    h = h_scr[...]
    u = _silu(_dot(h, w1_ref[...])) * _dot(h, w3_ref[...])
    acc_scr[...] += _dot(_bf(u), w2_ref[...])

    @pl.when(j == pl.num_programs(1) - 1)
    def _():
        _gated_residual(x_ref, acc_scr, pw_ref, gate_ref, o_ref, row0)


def _ffn(xc, pre_w, post_w, mods, w1, w3, w2, layer, tm):
    ntok, d = xc.shape
    dff = w1.shape[2]
    return pl.pallas_call(
        _ffn_kernel,
        out_shape=jax.ShapeDtypeStruct((ntok, d), F32),
        grid=(ntok // tm, dff // FF_TN),
        in_specs=[
            pl.BlockSpec((tm, d), lambda i, j: (i, 0), pipeline_mode=pl.Buffered(1)),
            pl.BlockSpec((1, d), lambda i, j: (0, 0)),
            pl.BlockSpec((8, d), lambda i, j: (0, 3)),
            pl.BlockSpec((8, d), lambda i, j: (0, 4)),
            pl.BlockSpec((8, d), lambda i, j: (0, 5)),
            pl.BlockSpec((1, d), lambda i, j: (0, 0)),
            pl.BlockSpec((None, d, FF_TN), lambda i, j: (layer, 0, j)),
            pl.BlockSpec((None, d, FF_TN), lambda i, j: (layer, 0, j)),
            pl.BlockSpec((None, FF_TN, d), lambda i, j: (layer, j, 0)),
        ],
        out_specs=pl.BlockSpec((tm, d), lambda i, j: (i, 0), pipeline_mode=pl.Buffered(1)),
        scratch_shapes=[pltpu.VMEM((tm, d), BF16), pltpu.VMEM((tm, d), F32)],
        compiler_params=_cparams(("arbitrary", "arbitrary")),
        name="ffn",
    )(xc, pre_w.reshape(1, d), mods, mods, mods, post_w.reshape(1, d), w1, w3, w2)


def _permute_w_in(w):
    main = [w[..., 0:1536], w[..., 1568:2080], w[..., 6720:7744], w[..., 3616:6688], w[..., 2080:3616],
            w[..., 7744:13888]]
    small = [w[..., 1536:1568], w[..., 6688:6720], jnp.zeros(w.shape[:2] + (SM_W - 64,), w.dtype)]
    return _bf(jnp.concatenate(main, axis=-1)), _bf(jnp.concatenate(small, axis=-1))


def _rope_tables(seq):
    rows = seq // GRID_W
    row_ids = jnp.repeat(jnp.arange(rows, dtype=F32), GRID_W)
    col_ids = jnp.tile(jnp.arange(GRID_W, dtype=F32), rows)
    half = DIFF_DQK // 2
    inv = 1.0 / (ROPE_BASE ** (jnp.arange(0, half, 2, dtype=F32) / half))
    ang_r = row_ids[:, None] * inv
    ang_c = col_ids[:, None] * inv
    cr, sr, cc, sc = jnp.cos(ang_r), jnp.sin(ang_r), jnp.cos(ang_c), jnp.sin(ang_c)
    cos_lat = jnp.tile(jnp.concatenate([cr, cr, cc, cc], axis=1), (1, 2))
    sin_lat = jnp.tile(jnp.concatenate([-sr, sr, -sc, sc], axis=1), (1, 2))
    cos_t = jnp.concatenate([jnp.ones((CTX_LEN, HEAD), F32), cos_lat], axis=0)
    sin_t = jnp.concatenate([jnp.zeros((CTX_LEN, HEAD), F32), sin_lat], axis=0)
    return cos_t, sin_t


def _lane_params(a_log, dt_bias):
    flat_a = a_log.reshape(-1)
    flat_b = dt_bias.reshape(-1)
    n = flat_a.shape[0]
    pa = jnp.zeros((HEAD,), F32).at[SM_A:SM_A + n].set(flat_a)
    pb = jnp.zeros((HEAD,), F32).at[SM_A:SM_A + n].set(flat_b)
    prow = jnp.zeros((8, HEAD), F32).at[0].set(pa).at[1].set(pb)
    pcol = jnp.zeros((HEAD, HEAD), F32).at[:, 0].set(pa).at[:, 1].set(pb)
    return prow, pcol


def _gla_gate_weights(w2, bias, dirn):
    wpad = jnp.zeros((HEAD, GLA_W), F32).at[GLA_GATE_RANK * dirn:GLA_GATE_RANK * (dirn + 1)].set(w2[dirn])
    return _bf(wpad), bias[dirn].reshape(1, GLA_W)


def kernel(x, c, ctx, c_ctx, ada_w, ada_b, mix_pre_w, mix_post_w, ffn_pre_w, ffn_post_w, w_in, gla_gate_w2, gla_gate_b, gla_norm_w, diff_lambda, diff_norm_w, delta_conv_w, delta_a_log, delta_dt_bias, delta_norm_w, w_up_gla, w_up_diff, w_up_delta, w_o, ffn_w1, ffn_w3, ffn_w2):
    seq = x.shape[1]
    ntok = CTX_LEN + seq
    tm = ROW_TILE if ntok % ROW_TILE == 0 else SCAN_ROWS
    in_tm = IN_TM if ntok % IN_TM == 0 else SCAN_ROWS
    xc = jnp.concatenate([ctx[0], x[0]], axis=0)
    cond = jnp.concatenate([c, c_ctx[None, :], jnp.zeros((COND_ROWS - 2, D_MODEL), F32)], axis=0)
    mods_all = _ada(_bf(jax.nn.silu(cond)), ada_w, ada_b)
    cos_t, sin_t = _rope_tables(seq)
    kv_tile = ROW_TILE if ntok % ROW_TILE == 0 else SCAN_ROWS
    w_main, w_small = _permute_w_in(w_in)
    wa, wd, we, wo = _bf(w_up_gla), _bf(w_up_diff), _bf(w_up_delta), _bf(w_o)
    w1, w3, w2 = _bf(ffn_w1), _bf(ffn_w3), _bf(ffn_w2)

    for layer in range(DEPTH):
        lam_init = 0.8 - 0.6 * math.exp(-0.3 * layer)
        mods = mods_all[layer]
        z, zs = _inproj(xc, mix_pre_w[layer], mods, w_main, w_small, layer, in_tm)

        oa = []
        for dirn in range(2):
            w2pad, bias = _gla_gate_weights(gla_gate_w2[layer], gla_gate_b[layer], dirn)
            oa.append(_gla(z, zs, w2pad, bias, reverse=bool(dirn)))

        qr, kr, vr = _rope(z, cos_t, sin_t)
        od = _attn(diff_lambda[layer], qr, kr, vr, kv_tile, lam_init)

        qkv = _conv(z, delta_conv_w[layer])
        zt = zs[:, 0:HEAD].T
        prow, pcol = _lane_params(delta_a_log[layer], delta_dt_bias[layer])
        oe = [_delta(qkv, zs, zt, prow, pcol, reverse=bool(dirn)) for dirn in range(2)]

        y = _up(oa[0], oa[1], od, oe[0], oe[1], z, gla_norm_w[layer], diff_norm_w[layer], delta_norm_w[layer],
                wa, wd, we, layer, lam_init, tm)
        xc = _oproj(y, xc, wo, mix_post_w[layer], mods, layer, tm)
        xc = _ffn(xc, ffn_pre_w[layer], ffn_post_w[layer], mods, w1, w3, w2, layer, tm)
    return xc[CTX_LEN:][None]
```

```python
import functools
import math

import numpy as np
import jax
import jax.numpy as jnp
from jax import lax
from jax.experimental import pallas as pl
from jax.experimental.pallas import tpu as pltpu

F32 = jnp.float32
BF16 = jnp.bfloat16

D_MODEL = 2048
DEPTH = 2
GRID_W = 64
CTX_LEN = 256
HEAD = 128
GLA_HEADS = 4
GLA_GATE_RANK = 16
GLA_GATE_NORM = 16.0
DIFF_HEADS = 4
DIFF_DQK = 64
DELTA_HEADS = 8
DELTA_CONV = 5
CHUNK = 64
ROPE_BASE = 10000.0
RMS_EPS = 1e-6
L2_EPS = 1e-6
D_FF = 5632
GLA_W = GLA_HEADS * HEAD
DIFF_W = DIFF_HEADS * HEAD
DELTA_W = DELTA_HEADS * HEAD

COL_GQ, COL_GK, COL_GV, COL_GG = 0, 512, 1024, 1536
COL_EG = 2048
COL_EQ, COL_EK, COL_EV = 3072, 4096, 5120
COL_DQ, COL_DK, COL_DV = 6144, 6656, 7168
COL_MG = 7680
NZ = 13824
SM_W = 256
SM_A = 32
SM_B = 48

ROW_TILE = 768
IN_TM = 1056
SCAN_ROWS = 256
IN_TN = 1536
FF_TN = 512
UP_TN = 512
ATT_TQ = 256
COND_ROWS = 16
CONV_HALO = 16
VMEM_LIMIT = 56 * 1024 * 1024


def _cparams(sem, vmem=VMEM_LIMIT):
    return pltpu.CompilerParams(dimension_semantics=sem, vmem_limit_bytes=vmem)


def _bf(x):
    return x.astype(BF16)


def _dot(a, b):
    return jnp.dot(a, b, preferred_element_type=F32)


def _dot_nt(a, b):
    return lax.dot_general(a, b, (((1,), (1,)), ((), ())), preferred_element_type=F32)


def _dot_tn(a, b):
    return lax.dot_general(a, b, (((0,), (0,)), ((), ())), preferred_element_type=F32)


def _split3(x):
    hi = _bf(x)
    r = x - hi.astype(F32)
    mid = _bf(r)
    lo = _bf(r - mid.astype(F32))
    return hi, mid, lo


def _dot_exact_lhs(m_bf, x):
    hi, mid, lo = _split3(x)
    return _dot(m_bf, hi) + _dot(m_bf, mid) + _dot(m_bf, lo)


def _dot_exact_rhs(x, m_bf):
    hi, mid, lo = _split3(x)
    return _dot(hi, m_bf) + _dot(mid, m_bf) + _dot(lo, m_bf)


def _sigmoid(x):
    return 1.0 / (1.0 + jnp.exp(-x))


def _silu(x):
    return x * _sigmoid(x)


def _softplus(x):
    return jnp.maximum(x, 0.0) + jnp.log1p(jnp.exp(-jnp.abs(x)))


def _log_sigmoid(x):
    return jnp.minimum(x, 0.0) - jnp.log1p(jnp.exp(-jnp.abs(x)))


def _rms(x, w):
    return x * lax.rsqrt(jnp.mean(x * x, axis=-1, keepdims=True) + RMS_EPS) * w


ROW_CHUNK = 16
ROW_UNROLL = 6


def _mod_row(mod_ref, row0):
    return mod_ref[pl.ds((row0 < CTX_LEN).astype(jnp.int32), 1), :]


def _norm_modulate(x_ref, nw_ref, sh_ref, sc_ref, h_ref, row0):
    def body(r, carry):
        rows = pl.ds(pl.multiple_of(r * ROW_CHUNK, ROW_CHUNK), ROW_CHUNK)
        g0 = row0 + r * ROW_CHUNK
        h = _rms(x_ref[rows, :], nw_ref[...])
        h_ref[rows, :] = _bf(h * (1.0 + _mod_row(sc_ref, g0)) + _mod_row(sh_ref, g0))
        return carry

    lax.fori_loop(0, x_ref.shape[0] // ROW_CHUNK, body, 0, unroll=ROW_UNROLL)


def _gated_residual(x_ref, t_ref, nw_ref, gate_ref, o_ref, row0):
    def body(r, carry):
        rows = pl.ds(pl.multiple_of(r * ROW_CHUNK, ROW_CHUNK), ROW_CHUNK)
        g0 = row0 + r * ROW_CHUNK
        o_ref[rows, :] = x_ref[rows, :] + _mod_row(gate_ref, g0) * _rms(t_ref[rows, :], nw_ref[...])
        return carry

    lax.fori_loop(0, x_ref.shape[0] // ROW_CHUNK, body, 0, unroll=ROW_UNROLL)


def _ada_kernel(s_ref, w_ref, b_ref, o_ref):
    o_ref[0] = _dot(s_ref[...], _bf(w_ref[0])) + b_ref[0]


def _ada(s_bf, ada_w, ada_b):
    depth, d, n6 = ada_w.shape
    tn = 1024
    return pl.pallas_call(
        _ada_kernel,
        out_shape=jax.ShapeDtypeStruct((depth, COND_ROWS, n6), F32),
        grid=(depth, n6 // tn),
        in_specs=[
            pl.BlockSpec((COND_ROWS, d), lambda l, j: (0, 0)),
            pl.BlockSpec((1, d, tn), lambda l, j: (l, 0, j)),
            pl.BlockSpec((1, 1, tn), lambda l, j: (l, 0, j)),
        ],
        out_specs=pl.BlockSpec((1, COND_ROWS, tn), lambda l, j: (l, 0, j)),
        compiler_params=_cparams(("arbitrary", "arbitrary")),
        name="ada",
    )(s_bf, ada_w, ada_b.reshape(depth, 1, n6))


def _inproj_kernel(x_ref, nw_ref, sh_ref, sc_ref, w_ref, ws_ref, o_ref, os_ref, h_scr):
    @pl.when(pl.program_id(1) == 0)
    def _():
        _norm_modulate(x_ref, nw_ref, sh_ref, sc_ref, h_scr, pl.program_id(0) * x_ref.shape[0])
        os_ref[...] = _dot(h_scr[...], ws_ref[...])

    o_ref[...] = _bf(_dot(h_scr[...], w_ref[...]))


def _inproj(xc, norm_w, mods, w_bf, ws_bf, layer, tm):
    ntok, d = xc.shape
    nz = w_bf.shape[2]
    return pl.pallas_call(
        _inproj_kernel,
        out_shape=(jax.ShapeDtypeStruct((ntok, nz), BF16), jax.ShapeDtypeStruct((ntok, SM_W), F32)),
        grid=(ntok // tm, nz // IN_TN),
        in_specs=[
            pl.BlockSpec((tm, d), lambda i, j: (i, 0), pipeline_mode=pl.Buffered(1)),
            pl.BlockSpec((1, d), lambda i, j: (0, 0)),
            pl.BlockSpec((8, d), lambda i, j: (0, 0)),
            pl.BlockSpec((8, d), lambda i, j: (0, 1)),
            pl.BlockSpec((None, d, IN_TN), lambda i, j: (layer, 0, j)),
            pl.BlockSpec((None, d, SM_W), lambda i, j: (layer, 0, 0)),
        ],
        out_specs=(pl.BlockSpec((tm, IN_TN), lambda i, j: (i, j)), pl.BlockSpec((tm, SM_W), lambda i, j: (i, 0))),
        scratch_shapes=[pltpu.VMEM((tm, d), BF16)],
        compiler_params=_cparams(("arbitrary", "arbitrary")),
        name="inproj",
    )(xc, norm_w.reshape(1, d), mods, mods, w_bf, ws_bf)


def _gla_consts(reverse):
    c = CHUNK
    i = np.arange(c)[:, None]
    t = np.arange(c)[None, :]
    if not reverse:
        mats = [t <= i, t > i]
    else:
        mats = [t >= i, t < i]
    masks = [i == t]
    for m in (32, 16, 8, 4, 2, 1):
        p = (i // (2 * m)) * (2 * m) + m - 1
        second = ((i // m) % 2) == 1
        same = (i // (2 * m)) == (t // (2 * m))
        t_second = ((t // m) % 2) == 1
        if not reverse:
            w = np.where(second, (t > p) & (t <= i), (t > i) & (t <= p))
            pm = same & second & ~t_second
        else:
            w = np.where(second, (t >= p + 1) & (t <= i - 1), (t >= i) & (t <= p))
            pm = same & ~second & t_second
        mats.append(w)
        masks.append(pm)
    wall = np.concatenate([m_.astype(np.float32) for m_ in mats], axis=0)
    pmask = np.stack([m_.astype(np.float32) for m_ in masks], axis=0)
    return jnp.asarray(wall, BF16), jnp.asarray(pmask, F32)


def _gla_kernel(qf_ref, kf_ref, vf_ref, smf_ref, qb_ref, kb_ref, vb_ref, smb_ref, w2_ref, b_ref, wall_ref, pm_ref,
                of_ref, ob_ref, st_scr, g_scr):
    @pl.when(pl.program_id(0) == 0)
    def _():
        st_scr[...] = jnp.zeros_like(st_scr)

    streams = ((qf_ref, kf_ref, vf_ref, smf_ref, of_ref), (qb_ref, kb_ref, vb_ref, smb_ref, ob_ref))
    n_chunks = qf_ref.shape[0] // CHUNK
    scale = HEAD ** -0.5
    for d, (_, _, _, sm_ref, _) in enumerate(streams):
        x = _dot(_bf(sm_ref[:, 0:HEAD]), w2_ref[d]) + b_ref[d]
        g_scr[d] = _log_sigmoid(x) * (1.0 / GLA_GATE_NORM)

    units = []
    for d, (q_ref, k_ref, v_ref, _, _) in enumerate(streams):
        for cc in range(n_chunks):
            rows = slice(cc * CHUNK, (cc + 1) * CHUNK)
            e_all = jnp.exp(_dot_exact_lhs(wall_ref[d], g_scr[d, rows, :]))
            for h in range(GLA_HEADS):
                cols = slice(h * HEAD, (h + 1) * HEAD)
                units.append(((d, cc, h), q_ref[rows, cols].astype(F32) * scale, k_ref[rows, cols].astype(F32),
                              v_ref[rows, cols], e_all[:, cols]))
    att = [jnp.where(pm_ref[key[0], 0] > 0.0, _dot_nt(_bf(q), _bf(k)), 0.0) for (key, q, k, _, _) in units]
    for lv in range(6):
        lvl = slice((2 + lv) * CHUNK, (3 + lv) * CHUNK)
        att = [a + jnp.where(pm_ref[key[0], lv + 1] > 0.0, _dot_nt(_bf(q * e[lvl]), _bf(k * e[lvl])), 0.0)
               for a, (key, q, k, _, e) in zip(att, units)]
    o_intra = {key: _dot(_bf(a), vb) for a, (key, _, _, vb, _) in zip(att, units)}
    q_in = {key: _bf(q * e[0:CHUNK]) for (key, q, _, _, e) in units}
    k_st = {key: _bf(k * e[CHUNK:2 * CHUNK]) for (key, _, k, _, e) in units}
    v_bf = {key: vb for (key, _, _, vb, _) in units}
    last = (CHUNK - 1, 0)
    e_last = {key: e[last[key[0]]:last[key[0]] + 1, :] for (key, _, _, _, e) in units}

    st = [st_scr[i] for i in range(2 * GLA_HEADS)]
    for ci in range(n_chunks):
        for d in range(2):
            cc = ci if d == 0 else n_chunks - 1 - ci
            o_ref = streams[d][4]
            for h in range(GLA_HEADS):
                key, si = (d, cc, h), d * GLA_HEADS + h
                o_ref[cc * CHUNK:(cc + 1) * CHUNK, h * HEAD:(h + 1) * HEAD] = _bf(
                    _dot_nt(q_in[key], _bf(st[si])) + o_intra[key])
                st[si] = st[si] * e_last[key] + _dot_tn(v_bf[key], k_st[key])
    for i in range(2 * GLA_HEADS):
        st_scr[i] = st[i]


def _scan_block_index(i, nblk, reverse):
    if not reverse:
        return i
    nctx = CTX_LEN // SCAN_ROWS
    return jnp.where(i < nctx, nctx - 1 - i, nblk - 1 + nctx - i)


def _full_spec(a):
    return pl.BlockSpec(a.shape, lambda i: (0,) * a.ndim)


def _gla(z, zs, w2pad, bias):
    ntok = z.shape[0]
    nblk = ntok // SCAN_ROWS
    consts = [_gla_consts(rev) for rev in (False, True)]
    wall = jnp.stack([c_[0] for c_ in consts])
    pmask = jnp.stack([c_[1] for c_ in consts])
    rbs = [functools.partial(_scan_block_index, nblk=nblk, reverse=rev) for rev in (False, True)]
    zspec = lambda rb, col: pl.BlockSpec((SCAN_ROWS, GLA_W), lambda i: (rb(i), col // GLA_W))
    stream = lambda rb: [zspec(rb, COL_GQ), zspec(rb, COL_GK), zspec(rb, COL_GV),
                         pl.BlockSpec((SCAN_ROWS, SM_W), lambda i: (rb(i), 0))]
    out = jax.ShapeDtypeStruct((ntok, GLA_W), BF16)
    return pl.pallas_call(
        _gla_kernel,
        out_shape=(out, out),
        grid=(nblk,),
        in_specs=stream(rbs[0]) + stream(rbs[1]) + [_full_spec(w2pad), _full_spec(bias), _full_spec(wall),
                                                     _full_spec(pmask)],
        out_specs=tuple(pl.BlockSpec((SCAN_ROWS, GLA_W), lambda i, rb=rb: (rb(i), 0)) for rb in rbs),
        scratch_shapes=[pltpu.VMEM((2 * GLA_HEADS, HEAD, HEAD), F32), pltpu.VMEM((2, SCAN_ROWS, GLA_W), F32)],
        compiler_params=_cparams(("arbitrary",)),
        name="gla",
    )(z, z, z, zs, z, z, z, zs, w2pad, bias, wall, pmask)


def _rope_kernel(q_ref, k_ref, v_ref, cos_ref, sin_ref, qo_ref, ko_ref, vo_ref):
    w = q_ref.shape[1]
    cos = jnp.concatenate([cos_ref[...]] * DIFF_HEADS, axis=1)
    sin = jnp.concatenate([sin_ref[...]] * DIFF_HEADS, axis=1)
    lane = lax.broadcasted_iota(jnp.int32, q_ref.shape, 1)
    first = (lane % 32) < 16

    def rot(x):
        swapped = jnp.where(first, pltpu.roll(x, w - 16, axis=1), pltpu.roll(x, 16, axis=1))
        return x * cos + swapped * sin

    qo_ref[...] = _bf(rot(q_ref[...].astype(F32)) * (DIFF_DQK ** -0.5 * math.log2(math.e)))
    ko_ref[...] = _bf(rot(k_ref[...].astype(F32)))
    lane_h = lax.broadcasted_iota(jnp.int32, (q_ref.shape[0], HEAD), 1)
    ones_col = jnp.where(lane_h == 0, 1.0, 0.0).astype(BF16)
    for h in range(DIFF_HEADS):
        vo_ref[:, 2 * h * HEAD:(2 * h + 1) * HEAD] = v_ref[:, h * HEAD:(h + 1) * HEAD]
        vo_ref[:, (2 * h + 1) * HEAD:(2 * h + 2) * HEAD] = ones_col


def _rope(z, cos_t, sin_t):
    ntok = z.shape[0]
    tm = SCAN_ROWS
    spec = lambda col: pl.BlockSpec((tm, DIFF_W), lambda i: (i, col // DIFF_W))
    out = jax.ShapeDtypeStruct((ntok, DIFF_W), BF16)
    return pl.pallas_call(
        _rope_kernel,
        out_shape=(out, out, jax.ShapeDtypeStruct((ntok, 2 * DIFF_W), BF16)),
        grid=(ntok // tm,),
        in_specs=[spec(COL_DQ), spec(COL_DK), spec(COL_DV),
                  pl.BlockSpec((tm, HEAD), lambda i: (i, 0)),
                  pl.BlockSpec((tm, HEAD), lambda i: (i, 0))],
        out_specs=(pl.BlockSpec((tm, DIFF_W), lambda i: (i, 0)), pl.BlockSpec((tm, DIFF_W), lambda i: (i, 0)),
                   pl.BlockSpec((tm, 2 * DIFF_W), lambda i: (i, 0))),
        compiler_params=_cparams(("arbitrary",)),
        name="rope",
    )(z, z, z, cos_t, sin_t)


def _attn_kernel(lam_ref, q_ref, k_ref, v_ref, o_ref, *, kv_tiles_ctx, kv_tiles_all, lam_init):
    tq = q_ref.shape[0]
    q = q_ref[...]
    lane = lax.broadcasted_iota(jnp.int32, q.shape, 1)
    zero = jnp.zeros_like(q)
    q2 = jnp.concatenate([jnp.where(lane < DIFF_DQK, q, zero), jnp.where(lane >= DIFF_DQK, q, zero)], axis=0)
    lp = lam_ref[...]
    lam = (jnp.exp(jnp.sum(lp[0:1] * lp[1:2], axis=-1, keepdims=True))
           - jnp.exp(jnp.sum(lp[2:3] * lp[3:4], axis=-1, keepdims=True)) + lam_init)

    def attend(kv_tiles):
        m = jnp.full((2 * tq, 1), -jnp.inf, F32)
        acc = jnp.zeros((2 * tq, 2 * HEAD), F32)
        for r0, r1 in kv_tiles:
            s = _dot_nt(q2, k_ref[r0:r1, :])
            m_new = jnp.maximum(m, jnp.max(s, axis=-1, keepdims=True))
            acc = jnp.exp2(m - m_new) * acc + _dot(_bf(jnp.exp2(s - m_new)), v_ref[r0:r1, :])
            m = m_new
        o = acc[:, 0:HEAD] / acc[:, HEAD:HEAD + 1]
        o_ref[...] = _bf(o[0:tq] - lam * o[tq:2 * tq])

    is_ctx = pl.program_id(1) < CTX_LEN // tq

    @pl.when(is_ctx)
    def _():
        attend(kv_tiles_ctx)

    @pl.when(jnp.logical_not(is_ctx))
    def _():
        attend(kv_tiles_all)


def _kv_tiles(n, tile):
    return tuple((r, min(r + tile, n)) for r in range(0, n, tile))


def _attn(lam_p, q, k, v, kv_tile, lam_init):
    ntok = q.shape[0]
    tq = ATT_TQ
    return pl.pallas_call(
        functools.partial(_attn_kernel, kv_tiles_ctx=_kv_tiles(CTX_LEN, kv_tile), kv_tiles_all=_kv_tiles(ntok, kv_tile),
                          lam_init=lam_init),
        out_shape=jax.ShapeDtypeStruct((ntok, DIFF_W), BF16),
        grid=(DIFF_HEADS, ntok // tq),
        in_specs=[
            pl.BlockSpec(lam_p.shape, lambda h, i: (0, 0)),
            pl.BlockSpec((tq, HEAD), lambda h, i: (i, h)),
            pl.BlockSpec((ntok, HEAD), lambda h, i: (0, h)),
            pl.BlockSpec((ntok, 2 * HEAD), lambda h, i: (0, h)),
        ],
        out_specs=pl.BlockSpec((tq, HEAD), lambda h, i: (i, h)),
        compiler_params=_cparams(("arbitrary", "arbitrary")),
        name="diff_attn",
    )(lam_p, q, k, v)


def _conv_kernel(xp_ref, x_ref, xn_ref, w_ref, o_ref):
    i = pl.program_id(0)
    nblk = pl.num_programs(0)
    nctx = CTX_LEN // SCAN_ROWS
    tm = x_ref.shape[0]
    pad = DELTA_CONV // 2
    has_prev = jnp.logical_and(i != 0, i != nctx)
    has_next = jnp.logical_and(i != nctx - 1, i != nblk - 1)
    prev = jnp.where(has_prev, xp_ref[...].astype(F32), 0.0)
    nxt = jnp.where(has_next, xn_ref[...].astype(F32), 0.0)
    xe = jnp.concatenate([prev, x_ref[...].astype(F32), nxt], axis=0)
    acc = None
    for j in range(DELTA_CONV):
        off = CONV_HALO + j - pad
        term = xe[off:off + tm, :] * w_ref[j:j + 1, :]
        acc = term if acc is None else acc + term
    y = _silu(acc)
    jc = pl.program_id(1)
    q_blocks = DELTA_W // x_ref.shape[1]
    is_qk = jc < 2 * q_blocks
    post = jnp.where(jc < q_blocks, HEAD ** -0.5, 1.0)
    for hh in range(x_ref.shape[1] // HEAD):
        cols = slice(hh * HEAD, (hh + 1) * HEAD)
        yh = y[:, cols]
        yn = yh * lax.rsqrt(jnp.sum(yh * yh, axis=-1, keepdims=True) + L2_EPS) * post
        o_ref[:, cols] = _bf(jnp.where(is_qk, yn, yh))


def _conv(z, conv_w):
    ntok = z.shape[0]
    tm = SCAN_ROWS
    width = 3 * DELTA_W
    nbh = ntok // CONV_HALO
    rh = tm // CONV_HALO
    cw = DELTA_W
    ncb = width // cw
    c0 = COL_EQ // cw
    return pl.pallas_call(
        _conv_kernel,
        out_shape=jax.ShapeDtypeStruct((ntok, width), BF16),
        grid=(ntok // tm, ncb),
        in_specs=[
            pl.BlockSpec((CONV_HALO, cw), lambda i, j: (jnp.maximum(i * rh - 1, 0), c0 + j)),
            pl.BlockSpec((tm, cw), lambda i, j: (i, c0 + j)),
            pl.BlockSpec((CONV_HALO, cw), lambda i, j: (jnp.minimum((i + 1) * rh, nbh - 1), c0 + j)),
            pl.BlockSpec((DELTA_CONV, cw), lambda i, j: (0, j)),
        ],
        out_specs=pl.BlockSpec((tm, cw), lambda i, j: (i, j)),
        compiler_params=_cparams(("arbitrary", "arbitrary")),
        name="delta_conv",
    )(z, z, z, conv_w)


def _delta_consts(reverse):
    c = CHUNK
    i = np.arange(c)[:, None]
    t = np.arange(c)[None, :]
    tri = (t >= i) if reverse else (t <= i)
    incl = tri
    strict = (t > i) if reverse else (t < i)
    cc = np.arange(4 * c)[None, :]
    j = cc % c
    eye = (i == j)
    m16 = (i // 16) == (j // 16)
    q32 = ((i // 32) == (j // 32)) & ~m16
    q64 = (i // 32) != (j // 32)
    r = np.arange(4 * c)[:, None]
    bd = (r // c) == (cc // c)
    f = lambda a: jnp.asarray(a.astype(np.float32))
    return dict(tri=jnp.asarray(tri.astype(np.float32), BF16), trit=jnp.asarray(tri.T.astype(np.float32), BF16),
                incl=f(incl), strict=f(strict), eye=f(eye), m16=f(m16), q32=f(q32), q64=f(q64),
                bd=jnp.asarray(bd.astype(np.float32), BF16))


def _blockdiag(xcat, bd):
    return jnp.concatenate([xcat] * 4, axis=0) * bd


def _delta_kernel(qf_ref, kf_ref, vf_ref, smf_ref, smtf_ref, qb_ref, kb_ref, vb_ref, smb_ref, smtb_ref, pr_ref, pc_ref,
                  tri_ref, trit_ref, incl_ref, strict_ref, eye_ref, m16_ref, q32_ref, q64_ref, bd_ref,
                  of_ref, ob_ref, s_scr, g_scr, b_scr, gt_scr):
    @pl.when(pl.program_id(0) == 0)
    def _():
        s_scr[...] = jnp.zeros_like(s_scr)

    streams = ((qf_ref, kf_ref, vf_ref, smf_ref, smtf_ref, of_ref), (qb_ref, kb_ref, vb_ref, smb_ref, smtb_ref, ob_ref))
    n_chunks = qf_ref.shape[0] // CHUNK
    for d, (_, _, _, sm_ref, smt_ref, _) in enumerate(streams):
        sm = sm_ref[:, 0:HEAD]
        g_scr[d] = -jnp.exp(pr_ref[0:1, :]) * _softplus(sm + pr_ref[1:2, :])
        b_scr[d] = _sigmoid(sm)
        gt = -jnp.exp(pc_ref[:, 0:1]) * _softplus(smt_ref[...] + pc_ref[:, 1:2])
        for c in range(n_chunks):
            gt_scr[d, c] = _dot_exact_rhs(gt[SM_A:SM_A + 2 * DELTA_HEADS, c * CHUNK:(c + 1) * CHUNK], trit_ref[d])
    last = (CHUNK - 1, 0)
    bd = bd_ref[...]

    def mm(a, b):
        return _dot(_bf(a), _blockdiag(_bf(b), bd))

    local = {}
    units = []
    for d, (q_ref, k_ref, v_ref, _, _, _) in enumerate(streams):
        incl = incl_ref[d] > 0.0
        strict = strict_ref[d] > 0.0
        for cc in range(n_chunks):
            rows = slice(cc * CHUNK, (cc + 1) * CHUNK)
            gc = _dot_exact_lhs(tri_ref[d], g_scr[d, rows, :])
            beta = b_scr[d, rows, :]
            gct = gt_scr[d, cc]
            for grp in range(DELTA_HEADS // 4):
                a_list, att_list, rhs_list, misc = [], [], [], []
                for hh in range(4):
                    h = grp * 4 + hh
                    cols = slice(h * HEAD, (h + 1) * HEAD)
                    la = SM_A + DELTA_HEADS * d + h
                    lb = SM_B + DELTA_HEADS * d + h
                    gcol = gc[:, la:la + 1]
                    grow = gct[DELTA_HEADS * d + h:DELTA_HEADS * d + h + 1, :]
                    bcol = beta[:, lb:lb + 1]
                    q = q_ref[rows, cols].astype(F32)
                    k = k_ref[rows, cols].astype(F32)
                    v = v_ref[rows, cols].astype(F32)
                    dec = jnp.exp(jnp.where(incl, gcol - grow, -jnp.inf))
                    kb = k * bcol
                    kbf = _bf(k)
                    a_list.append(jnp.where(strict, _dot_nt(_bf(kb), kbf) * dec, 0.0))
                    att_list.append(_bf(_dot_nt(_bf(q), kbf) * dec))
                    eg = jnp.exp(gcol)
                    rhs_list.append(_bf(jnp.concatenate([v * bcol, kb * eg], axis=1)))
                    g_last = gcol[last[d]:last[d] + 1, :]
                    misc.append((_bf(q * eg), _bf(k * jnp.exp(g_last - gcol)), jnp.exp(g_last)))
                units.append(((d, cc, grp), jnp.concatenate(a_list, axis=1), att_list,
                              jnp.concatenate(rhs_list, axis=0), misc))

    acat = [u_[1] for u_ in units]
    n = [-(a * m16_ref[...]) for a in acat]
    p = [eye_ref[...] + n_ for n_ in n]
    for _ in range(3):
        n = [mm(n_, n_) for n_ in n]
        p = [p_ + mm(p_, n_) for p_, n_ in zip(p, n)]
    for q_ref_ in (q32_ref, q64_ref):
        x = [mm(p_, a * q_ref_[...]) for p_, a in zip(p, acat)]
        p = [p_ - mm(x_, p_) for p_, x_ in zip(p, x)]
    sol = [_dot(_blockdiag(_bf(p_), bd), u_[3]) for p_, u_ in zip(p, units)]
    for ((d, cc, grp), _, att_list, _, misc), sol_ in zip(units, sol):
        for hh in range(4):
            u = sol_[hh * CHUNK:(hh + 1) * CHUNK, 0:HEAD]
            w = _bf(sol_[hh * CHUNK:(hh + 1) * CHUNK, HEAD:2 * HEAD])
            local[(d, cc, grp * 4 + hh)] = (u, w, att_list[hh]) + misc[hh]

    s = [s_scr[i] for i in range(2 * DELTA_HEADS)]
    for ci in range(n_chunks):
        keys = [(d, ci if d == 0 else n_chunks - 1 - ci, h) for d in range(2) for h in range(DELTA_HEADS)]
        sb = [_bf(s_) for s_ in s]
        vnb = [_bf(local[key][0] - _dot(local[key][1], sb[i])) for i, key in enumerate(keys)]
        for i, key in enumerate(keys):
            d, cc, h = key
            _, _, att, qd, kd, egl = local[key]
            streams[d][5][cc * CHUNK:(cc + 1) * CHUNK, h * HEAD:(h + 1) * HEAD] = _bf(
                _dot(qd, sb[i]) + _dot(att, vnb[i]))
            s[i] = s[i] * egl + _dot_tn(kd, vnb[i])
    for i in range(2 * DELTA_HEADS):
        s_scr[i] = s[i]


def _delta(qkv, zs, zt, prow, pcol):
    ntok = zs.shape[0]
    nblk = ntok // SCAN_ROWS
    csts = [_delta_consts(rev) for rev in (False, True)]
    per_dir = [jnp.stack([c_[n] for c_ in csts]) for n in ("tri", "trit", "incl", "strict")]
    shared = [csts[0][n] for n in ("eye", "m16", "q32", "q64", "bd")]
    rbs = [functools.partial(_scan_block_index, nblk=nblk, reverse=rev) for rev in (False, True)]
    col = lambda rb, j: pl.BlockSpec((SCAN_ROWS, DELTA_W), lambda i: (rb(i), j))
    stream = lambda rb: [col(rb, 0), col(rb, 1), col(rb, 2), pl.BlockSpec((SCAN_ROWS, SM_W), lambda i: (rb(i), 0)),
                         pl.BlockSpec((HEAD, SCAN_ROWS), lambda i: (0, rb(i)))]
    consts = [prow, pcol] + per_dir + shared
    out = jax.ShapeDtypeStruct((ntok, DELTA_W), BF16)
    return pl.pallas_call(
        _delta_kernel,
        out_shape=(out, out),
        grid=(nblk,),
        in_specs=stream(rbs[0]) + stream(rbs[1]) + [_full_spec(a) for a in consts],
        out_specs=tuple(pl.BlockSpec((SCAN_ROWS, DELTA_W), lambda i, rb=rb: (rb(i), 0)) for rb in rbs),
        scratch_shapes=[pltpu.VMEM((2 * DELTA_HEADS, HEAD, HEAD), F32),
                        pltpu.VMEM((2, SCAN_ROWS, HEAD), F32),
                        pltpu.VMEM((2, SCAN_ROWS, HEAD), F32),
                        pltpu.VMEM((2, SCAN_ROWS // CHUNK, 2 * DELTA_HEADS, CHUNK), F32)],
        compiler_params=_cparams(("arbitrary",)),
        name="delta",
    )(qkv, qkv, qkv, zs, zt, qkv, qkv, qkv, zs, zt, *consts)


def _head_rms(x, w_ref, n_heads):
    parts = []
    for h in range(n_heads):
        xh = x[:, h * HEAD:(h + 1) * HEAD]
        parts.append(xh * lax.rsqrt(jnp.mean(xh * xh, axis=-1, keepdims=True) + RMS_EPS) * w_ref[...])
    return jnp.concatenate(parts, axis=1)


def _up_kernel(oaf_ref, oab_ref, od_ref, oef_ref, oeb_ref, gg_ref, eg_ref, ga_ref, gd_ref, ge_ref,
               nwa_ref, nwd_ref, nwe_ref, wa_ref, wd_ref, we_ref, y_ref, a_scr, d_scr, e_scr, *, lam_init):
    @pl.when(pl.program_id(1) == 0)
    def _():
        gg = gg_ref[...].astype(F32)
        eg = eg_ref[...].astype(F32)
        oa = oaf_ref[...].astype(F32) + oab_ref[...].astype(F32)
        oe = oef_ref[...].astype(F32) + oeb_ref[...].astype(F32)
        a_scr[...] = _bf(_head_rms(oa, nwa_ref, GLA_HEADS) * _silu(gg))
        d_scr[...] = _bf(_head_rms(od_ref[...].astype(F32), nwd_ref, DIFF_HEADS) * (1.0 - lam_init))
        e_scr[...] = _bf(_head_rms(oe, nwe_ref, DELTA_HEADS) * _silu(eg))

    y = _sigmoid(ga_ref[...].astype(F32)) * _dot(a_scr[...], wa_ref[...])
    y = y + _sigmoid(gd_ref[...].astype(F32)) * _dot(d_scr[...], wd_ref[...])
    y = y + _sigmoid(ge_ref[...].astype(F32)) * _dot(e_scr[...], we_ref[...])
    y_ref[...] = _bf(y)


def _up(oaf, oab, od, oef, oeb, z, nwa, nwd, nwe, wa, wd, we, layer, lam_init, tm):
    ntok = z.shape[0]
    tn = UP_TN
    rows = lambda w: pl.BlockSpec((tm, w), lambda i, j: (i, 0))
    zcol = lambda col, w: pl.BlockSpec((tm, w), lambda i, j: (i, col // w))
    gate = lambda b: pl.BlockSpec((tm, tn), lambda i, j: (i, (COL_MG + b * D_MODEL) // tn + j))
    vec = pl.BlockSpec((1, HEAD), lambda i, j: (0, 0))
    wspec = lambda k: pl.BlockSpec((None, k, tn), lambda i, j: (layer, 0, j))
    return pl.pallas_call(
        functools.partial(_up_kernel, lam_init=lam_init),
        out_shape=jax.ShapeDtypeStruct((ntok, D_MODEL), BF16),
        grid=(ntok // tm, D_MODEL // tn),
        in_specs=[rows(GLA_W), rows(GLA_W), rows(DIFF_W), rows(DELTA_W), rows(DELTA_W),
                  zcol(COL_GG, GLA_W), zcol(COL_EG, DELTA_W), gate(0), gate(1), gate(2),
                  vec, vec, vec, wspec(GLA_W), wspec(DIFF_W), wspec(DELTA_W)],
        out_specs=pl.BlockSpec((tm, tn), lambda i, j: (i, j)),
        scratch_shapes=[pltpu.VMEM((tm, GLA_W), BF16), pltpu.VMEM((tm, DIFF_W), BF16), pltpu.VMEM((tm, DELTA_W), BF16)],
        compiler_params=_cparams(("arbitrary", "arbitrary")),
        name="up_merge",
    )(oaf, oab, od, oef, oeb, z, z, z, z, z, nwa.reshape(1, HEAD), nwd.reshape(1, HEAD), nwe.reshape(1, HEAD),
      wa, wd, we)


def _oproj_kernel(y_ref, x_ref, wo_ref, nw_ref, gate_ref, o_ref, t_scr):
    t_scr[...] = _dot(y_ref[...], wo_ref[...])
    _gated_residual(x_ref, t_scr, nw_ref, gate_ref, o_ref, pl.program_id(0) * x_ref.shape[0])


def _oproj(y, xc, wo, norm_w, mods, layer, tm):
    ntok, d = xc.shape
    return pl.pallas_call(
        _oproj_kernel,
        out_shape=jax.ShapeDtypeStruct((ntok, d), F32),
        grid=(ntok // tm,),
        in_specs=[
            pl.BlockSpec((tm, d), lambda i: (i, 0)),
            pl.BlockSpec((tm, d), lambda i: (i, 0)),
            pl.BlockSpec((None, d, d), lambda i: (layer, 0, 0), pipeline_mode=pl.Buffered(1)),
            pl.BlockSpec((1, d), lambda i: (0, 0)),
            pl.BlockSpec((8, d), lambda i: (0, 2)),
        ],
        out_specs=pl.BlockSpec((tm, d), lambda i: (i, 0)),
        scratch_shapes=[pltpu.VMEM((tm, d), F32)],
        compiler_params=_cparams(("arbitrary",)),
        name="out_proj",
    )(y, xc, wo, norm_w.reshape(1, d), mods)


def _ffn_kernel(x_ref, nw_ref, sh_ref, sc_ref, gate_ref, pw_ref, w1_ref, w3_ref, w2_ref, o_ref, h_scr, acc_scr):
    j = pl.program_id(1)
    row0 = pl.program_id(0) * x_ref.shape[0]

    @pl.when(j == 0)
    def _():
        _norm_modulate(x_ref, nw_ref, sh_ref, sc_ref, h_scr, row0)
        acc_scr[...] = jnp.zeros_like(acc_scr)

    h = h_scr[...]
    u = _silu(_dot(h, w1_ref[...])) * _dot(h, w3_ref[...])
    acc_scr[...] += _dot(_bf(u), w2_ref[...])

    @pl.when(j == pl.num_programs(1) - 1)
    def _():
        _gated_residual(x_ref, acc_scr, pw_ref, gate_ref, o_ref, row0)


def _ffn(xc, pre_w, post_w, mods, w1, w3, w2, layer, tm):
    ntok, d = xc.shape
    dff = w1.shape[2]
    return pl.pallas_call(
        _ffn_kernel,
        out_shape=jax.ShapeDtypeStruct((ntok, d), F32),
        grid=(ntok // tm, dff // FF_TN),
        in_specs=[
            pl.BlockSpec((tm, d), lambda i, j: (i, 0), pipeline_mode=pl.Buffered(1)),
            pl.BlockSpec((1, d), lambda i, j: (0, 0)),
            pl.BlockSpec((8, d), lambda i, j: (0, 3)),
            pl.BlockSpec((8, d), lambda i, j: (0, 4)),
            pl.BlockSpec((8, d), lambda i, j: (0, 5)),
            pl.BlockSpec((1, d), lambda i, j: (0, 0)),
            pl.BlockSpec((None, d, FF_TN), lambda i, j: (layer, 0, j)),
            pl.BlockSpec((None, d, FF_TN), lambda i, j: (layer, 0, j)),
            pl.BlockSpec((None, FF_TN, d), lambda i, j: (layer, j, 0)),
        ],
        out_specs=pl.BlockSpec((tm, d), lambda i, j: (i, 0), pipeline_mode=pl.Buffered(1)),
        scratch_shapes=[pltpu.VMEM((tm, d), BF16), pltpu.VMEM((tm, d), F32)],
        compiler_params=_cparams(("arbitrary", "arbitrary")),
        name="ffn",
    )(xc, pre_w.reshape(1, d), mods, mods, mods, post_w.reshape(1, d), w1, w3, w2)


def _permute_w_in(w):
    main = [w[..., 0:1536], w[..., 1568:2080], w[..., 6720:7744], w[..., 3616:6688], w[..., 2080:3616],
            w[..., 7744:13888]]
    small = [w[..., 1536:1568], w[..., 6688:6720], jnp.zeros(w.shape[:2] + (SM_W - 64,), w.dtype)]
    return _bf(jnp.concatenate(main, axis=-1)), _bf(jnp.concatenate(small, axis=-1))


def _rope_tables(seq):
    rows = seq // GRID_W
    row_ids = jnp.repeat(jnp.arange(rows, dtype=F32), GRID_W)
    col_ids = jnp.tile(jnp.arange(GRID_W, dtype=F32), rows)
    half = DIFF_DQK // 2
    inv = 1.0 / (ROPE_BASE ** (jnp.arange(0, half, 2, dtype=F32) / half))
    ang_r = row_ids[:, None] * inv
    ang_c = col_ids[:, None] * inv
    cr, sr, cc, sc = jnp.cos(ang_r), jnp.sin(ang_r), jnp.cos(ang_c), jnp.sin(ang_c)
    cos_lat = jnp.tile(jnp.concatenate([cr, cr, cc, cc], axis=1), (1, 2))
    sin_lat = jnp.tile(jnp.concatenate([-sr, sr, -sc, sc], axis=1), (1, 2))
    cos_t = jnp.concatenate([jnp.ones((CTX_LEN, HEAD), F32), cos_lat], axis=0)
    sin_t = jnp.concatenate([jnp.zeros((CTX_LEN, HEAD), F32), sin_lat], axis=0)
    return cos_t, sin_t


def _lane_params(a_log, dt_bias):
    flat_a = a_log.reshape(-1)
    flat_b = dt_bias.reshape(-1)
    n = flat_a.shape[0]
    pa = jnp.zeros((HEAD,), F32).at[SM_A:SM_A + n].set(flat_a)
    pb = jnp.zeros((HEAD,), F32).at[SM_A:SM_A + n].set(flat_b)
    prow = jnp.zeros((8, HEAD), F32).at[0].set(pa).at[1].set(pb)
    pcol = jnp.zeros((HEAD, HEAD), F32).at[:, 0].set(pa).at[:, 1].set(pb)
    return prow, pcol


def _gla_gate_weights(w2, bias):
    wpad = jnp.zeros((2, HEAD, GLA_W), F32)
    for dirn in range(2):
        wpad = wpad.at[dirn, GLA_GATE_RANK * dirn:GLA_GATE_RANK * (dirn + 1)].set(w2[dirn])
    return _bf(wpad), bias.reshape(2, 1, GLA_W)


def kernel(x, c, ctx, c_ctx, ada_w, ada_b, mix_pre_w, mix_post_w, ffn_pre_w, ffn_post_w, w_in, gla_gate_w2, gla_gate_b, gla_norm_w, diff_lambda, diff_norm_w, delta_conv_w, delta_a_log, delta_dt_bias, delta_norm_w, w_up_gla, w_up_diff, w_up_delta, w_o, ffn_w1, ffn_w3, ffn_w2):
    seq = x.shape[1]
    ntok = CTX_LEN + seq
    tm = ROW_TILE if ntok % ROW_TILE == 0 else SCAN_ROWS
    in_tm = IN_TM if ntok % IN_TM == 0 else SCAN_ROWS
    xc = jnp.concatenate([ctx[0], x[0]], axis=0)
    cond = jnp.concatenate([c, c_ctx[None, :], jnp.zeros((COND_ROWS - 2, D_MODEL), F32)], axis=0)
    mods_all = _ada(_bf(jax.nn.silu(cond)), ada_w, ada_b)
    cos_t, sin_t = _rope_tables(seq)
    kv_tile = ROW_TILE if ntok % ROW_TILE == 0 else SCAN_ROWS
    w_main, w_small = _permute_w_in(w_in)
    wa, wd, we, wo = _bf(w_up_gla), _bf(w_up_diff), _bf(w_up_delta), _bf(w_o)
    w1, w3, w2 = _bf(ffn_w1), _bf(ffn_w3), _bf(ffn_w2)

    for layer in range(DEPTH):
        lam_init = 0.8 - 0.6 * math.exp(-0.3 * layer)
        mods = mods_all[layer]
        z, zs = _inproj(xc, mix_pre_w[layer], mods, w_main, w_small, layer, in_tm)

        oa = _gla(z, zs, *_gla_gate_weights(gla_gate_w2[layer], gla_gate_b[layer]))

        qr, kr, vr = _rope(z, cos_t, sin_t)
        od = _attn(diff_lambda[layer], qr, kr, vr, kv_tile, lam_init)

        qkv = _conv(z, delta_conv_w[layer])
        zt = zs[:, 0:HEAD].T
        prow, pcol = _lane_params(delta_a_log[layer], delta_dt_bias[layer])
        oe = _delta(qkv, zs, zt, prow, pcol)

        y = _up(oa[0], oa[1], od, oe[0], oe[1], z, gla_norm_w[layer], diff_norm_w[layer], delta_norm_w[layer],
                wa, wd, we, layer, lam_init, tm)
        xc = _oproj(y, xc, wo, mix_post_w[layer], mods, layer, tm)
        xc = _ffn(xc, ffn_pre_w[layer], ffn_post_w[layer], mods, w1, w3, w2, layer, in_tm)
    return xc[CTX_LEN:][None]
```

```python
import functools
import math

import numpy as np
import jax
import jax.numpy as jnp
from jax import lax
from jax.experimental import pallas as pl
from jax.experimental.pallas import tpu as pltpu

F32 = jnp.float32
BF16 = jnp.bfloat16

D_MODEL = 2048
DEPTH = 2
GRID_W = 64
CTX_LEN = 256
HEAD = 128
GLA_HEADS = 4
GLA_GATE_RANK = 16
GLA_GATE_NORM = 16.0
DIFF_HEADS = 4
DIFF_DQK = 64
DELTA_HEADS = 8
DELTA_CONV = 5
CHUNK = 64
ROPE_BASE = 10000.0
RMS_EPS = 1e-6
L2_EPS = 1e-6
D_FF = 5632
GLA_W = GLA_HEADS * HEAD
DIFF_W = DIFF_HEADS * HEAD
DELTA_W = DELTA_HEADS * HEAD

COL_GQ, COL_GK, COL_GV, COL_GG = 0, 512, 1024, 1536
COL_EG = 2048
COL_EQ, COL_EK, COL_EV = 3072, 4096, 5120
COL_DQ, COL_DK, COL_DV = 6144, 6656, 7168
COL_MG = 7680
NZ = 13824
SM_W = 256
SM_A = 32
SM_B = 48

ROW_TILE = 768
IN_TM = 1056
SCAN_ROWS = 256
IN_TN = 1536
FF_TN = 512
UP_TN = 512
ATT_TQ = 256
COND_ROWS = 16
CONV_HALO = 16
GLA_GROUP = 32
VMEM_LIMIT = 56 * 1024 * 1024


def _cparams(sem, vmem=VMEM_LIMIT):
    return pltpu.CompilerParams(dimension_semantics=sem, vmem_limit_bytes=vmem)


def _bf(x):
    return x.astype(BF16)


def _dot(a, b):
    return jnp.dot(a, b, preferred_element_type=F32)


def _dot_nt(a, b):
    return lax.dot_general(a, b, (((1,), (1,)), ((), ())), preferred_element_type=F32)


def _dot_tn(a, b):
    return lax.dot_general(a, b, (((0,), (0,)), ((), ())), preferred_element_type=F32)


def _split3(x):
    hi = _bf(x)
    r = x - hi.astype(F32)
    mid = _bf(r)
    lo = _bf(r - mid.astype(F32))
    return hi, mid, lo


def _dot_exact_lhs(m3_bf, x):
    return _dot(m3_bf, jnp.concatenate(_split3(x), axis=0))


def _dot_exact_rhs(x, m3_bf):
    return _dot(jnp.concatenate(_split3(x), axis=1), m3_bf)


def _rep3(m, axis):
    return jnp.asarray(np.concatenate([m.astype(np.float32)] * 3, axis=axis), BF16)


def _sigmoid(x):
    return 1.0 / (1.0 + jnp.exp(-x))


def _silu(x):
    return x * _sigmoid(x)


def _softplus(x):
    return jnp.maximum(x, 0.0) + jnp.log1p(jnp.exp(-jnp.abs(x)))


def _log_sigmoid(x):
    return jnp.minimum(x, 0.0) - jnp.log1p(jnp.exp(-jnp.abs(x)))


def _rms(x, w):
    return x * lax.rsqrt(jnp.mean(x * x, axis=-1, keepdims=True) + RMS_EPS) * w


ROW_CHUNK = 16
ROW_UNROLL = 6


def _mod_row(mod_ref, row0):
    return mod_ref[pl.ds((row0 < CTX_LEN).astype(jnp.int32), 1), :]


def _norm_modulate(x_ref, nw_ref, sh_ref, sc_ref, h_ref, row0):
    def body(r, carry):
        rows = pl.ds(pl.multiple_of(r * ROW_CHUNK, ROW_CHUNK), ROW_CHUNK)
        g0 = row0 + r * ROW_CHUNK
        h = _rms(x_ref[rows, :], nw_ref[...])
        h_ref[rows, :] = _bf(h * (1.0 + _mod_row(sc_ref, g0)) + _mod_row(sh_ref, g0))
        return carry

    lax.fori_loop(0, x_ref.shape[0] // ROW_CHUNK, body, 0, unroll=ROW_UNROLL)


def _gated_residual(x_ref, t_ref, nw_ref, gate_ref, o_ref, row0):
    def body(r, carry):
        rows = pl.ds(pl.multiple_of(r * ROW_CHUNK, ROW_CHUNK), ROW_CHUNK)
        g0 = row0 + r * ROW_CHUNK
        o_ref[rows, :] = x_ref[rows, :] + _mod_row(gate_ref, g0) * _rms(t_ref[rows, :], nw_ref[...])
        return carry

    lax.fori_loop(0, x_ref.shape[0] // ROW_CHUNK, body, 0, unroll=ROW_UNROLL)


def _ada_kernel(s_ref, w_ref, b_ref, o_ref):
    o_ref[0] = _dot(s_ref[...], _bf(w_ref[0])) + b_ref[0]


def _ada(s_bf, ada_w, ada_b):
    depth, d, n6 = ada_w.shape
    tn = 1024
    return pl.pallas_call(
        _ada_kernel,
        out_shape=jax.ShapeDtypeStruct((depth, COND_ROWS, n6), F32),
        grid=(depth, n6 // tn),
        in_specs=[
            pl.BlockSpec((COND_ROWS, d), lambda l, j: (0, 0)),
            pl.BlockSpec((1, d, tn), lambda l, j: (l, 0, j)),
            pl.BlockSpec((1, 1, tn), lambda l, j: (l, 0, j)),
        ],
        out_specs=pl.BlockSpec((1, COND_ROWS, tn), lambda l, j: (l, 0, j)),
        compiler_params=_cparams(("arbitrary", "arbitrary")),
        name="ada",
    )(s_bf, ada_w, ada_b.reshape(depth, 1, n6))


def _inproj_kernel(x_ref, nw_ref, sh_ref, sc_ref, w_ref, ws_ref, o_ref, os_ref, h_scr):
    @pl.when(pl.program_id(1) == 0)
    def _():
        _norm_modulate(x_ref, nw_ref, sh_ref, sc_ref, h_scr, pl.program_id(0) * x_ref.shape[0])
        os_ref[...] = _dot(h_scr[...], ws_ref[...])

    o_ref[...] = _bf(_dot(h_scr[...], w_ref[...]))


def _inproj(xc, norm_w, mods, w_bf, ws_bf, layer, tm):
    ntok, d = xc.shape
    nz = w_bf.shape[2]
    return pl.pallas_call(
        _inproj_kernel,
        out_shape=(jax.ShapeDtypeStruct((ntok, nz), BF16), jax.ShapeDtypeStruct((ntok, SM_W), F32)),
        grid=(ntok // tm, nz // IN_TN),
        in_specs=[
            pl.BlockSpec((tm, d), lambda i, j: (i, 0)),
            pl.BlockSpec((1, d), lambda i, j: (0, 0)),
            pl.BlockSpec((8, d), lambda i, j: (0, 0)),
            pl.BlockSpec((8, d), lambda i, j: (0, 1)),
            pl.BlockSpec((None, d, IN_TN), lambda i, j: (layer, 0, j)),
            pl.BlockSpec((None, d, SM_W), lambda i, j: (layer, 0, 0)),
        ],
        out_specs=(pl.BlockSpec((tm, IN_TN), lambda i, j: (i, j)), pl.BlockSpec((tm, SM_W), lambda i, j: (i, 0))),
        scratch_shapes=[pltpu.VMEM((tm, d), BF16)],
        compiler_params=_cparams(("arbitrary", "arbitrary")),
        name="inproj",
    )(xc, norm_w.reshape(1, d), mods, mods, w_bf, ws_bf)


def _gla_consts(reverse):
    c = CHUNK
    i = np.arange(c)[:, None]
    t = np.arange(c)[None, :]
    if not reverse:
        mats = [t <= i, t > i]
    else:
        mats = [t >= i, t < i]
    masks = [i == t]
    for m in (32, 16, 8, 4, 2, 1):
        p = (i // (2 * m)) * (2 * m) + m - 1
        second = ((i // m) % 2) == 1
        same = (i // (2 * m)) == (t // (2 * m))
        t_second = ((t // m) % 2) == 1
        if not reverse:
            w = np.where(second, (t > p) & (t <= i), (t > i) & (t <= p))
            pm = same & second & ~t_second
        else:
            w = np.where(second, (t >= p + 1) & (t <= i - 1), (t >= i) & (t <= p))
            pm = same & ~second & t_second
        mats.append(w)
        masks.append(pm)
    wall = np.concatenate([m_.astype(np.float32) for m_ in mats], axis=0)
    pmask = np.stack([m_.astype(np.float32) for m_ in masks], axis=0)
    return _rep3(wall, 1), jnp.asarray(pmask, F32)


def _gla_kernel(qf_ref, kf_ref, vf_ref, smf_ref, qb_ref, kb_ref, vb_ref, smb_ref, w2_ref, b_ref, wall_ref, pm_ref,
                of_ref, ob_ref, st_scr, g_scr):
    @pl.when(pl.program_id(0) == 0)
    def _():
        st_scr[...] = jnp.zeros_like(st_scr)

    streams = ((qf_ref, kf_ref, vf_ref, smf_ref, of_ref), (qb_ref, kb_ref, vb_ref, smb_ref, ob_ref))
    n_chunks = qf_ref.shape[0] // CHUNK
    scale = HEAD ** -0.5
    for d, (_, _, _, sm_ref, _) in enumerate(streams):
        x = _dot(_bf(sm_ref[:, 0:HEAD]), w2_ref[d]) + b_ref[d]
        g_scr[d] = _log_sigmoid(x) * (1.0 / GLA_GATE_NORM)

    units = []
    for d, (q_ref, k_ref, v_ref, _, _) in enumerate(streams):
        for cc in range(n_chunks):
            rows = slice(cc * CHUNK, (cc + 1) * CHUNK)
            e_all = jnp.exp(_dot_exact_lhs(wall_ref[d], g_scr[d, rows, :]))
            for h in range(GLA_HEADS):
                cols = slice(h * HEAD, (h + 1) * HEAD)
                units.append(((d, cc, h), q_ref[rows, cols].astype(F32) * scale, k_ref[rows, cols].astype(F32),
                              v_ref[rows, cols], e_all[:, cols]))
    o_intra, q_in, k_st, v_bf, e_last = {}, {}, {}, {}, {}
    last = (CHUNK - 1, 0)
    for g0 in range(0, len(units), GLA_GROUP):
        grp = units[g0:g0 + GLA_GROUP]
        att = [jnp.where(pm_ref[key[0], 0] > 0.0, _dot_nt(_bf(q), _bf(k)), 0.0) for (key, q, k, _, _) in grp]
        for lv in range(6):
            lvl = slice((2 + lv) * CHUNK, (3 + lv) * CHUNK)
            att = [a + jnp.where(pm_ref[key[0], lv + 1] > 0.0, _dot_nt(_bf(q * e[lvl]), _bf(k * e[lvl])), 0.0)
                   for a, (key, q, k, _, e) in zip(att, grp)]
        for a, (key, q, k, vb, e) in zip(att, grp):
            o_intra[key] = _dot(_bf(a), vb)
            q_in[key] = _bf(q * e[0:CHUNK])
            k_st[key] = _bf(k * e[CHUNK:2 * CHUNK])
            v_bf[key] = vb
            e_last[key] = e[last[key[0]]:last[key[0]] + 1, :]

    st = [st_scr[i] for i in range(2 * GLA_HEADS)]
    for ci in range(n_chunks):
        for d in range(2):
            cc = ci if d == 0 else n_chunks - 1 - ci
            o_ref = streams[d][4]
            for h in range(GLA_HEADS):
                key, si = (d, cc, h), d * GLA_HEADS + h
                o_ref[cc * CHUNK:(cc + 1) * CHUNK, h * HEAD:(h + 1) * HEAD] = _bf(
                    _dot_nt(q_in[key], _bf(st[si])) + o_intra[key])
                st[si] = st[si] * e_last[key] + _dot_tn(v_bf[key], k_st[key])
    for i in range(2 * GLA_HEADS):
        st_scr[i] = st[i]


def _scan_block_index(i, nblk, reverse):
    if not reverse:
        return i
    nctx = CTX_LEN // SCAN_ROWS
    return jnp.where(i < nctx, nctx - 1 - i, nblk - 1 + nctx - i)


def _full_spec(a):
    return pl.BlockSpec(a.shape, lambda i: (0,) * a.ndim)


def _gla(z, zs, w2pad, bias):
    ntok = z.shape[0]
    nblk = ntok // SCAN_ROWS
    consts = [_gla_consts(rev) for rev in (False, True)]
    wall = jnp.stack([c_[0] for c_ in consts])
    pmask = jnp.stack([c_[1] for c_ in consts])
    rbs = [functools.partial(_scan_block_index, nblk=nblk, reverse=rev) for rev in (False, True)]
    zspec = lambda rb, col: pl.BlockSpec((SCAN_ROWS, GLA_W), lambda i: (rb(i), col // GLA_W))
    stream = lambda rb: [zspec(rb, COL_GQ), zspec(rb, COL_GK), zspec(rb, COL_GV),
                         pl.BlockSpec((SCAN_ROWS, SM_W), lambda i: (rb(i), 0))]
    out = jax.ShapeDtypeStruct((ntok, GLA_W), BF16)
    return pl.pallas_call(
        _gla_kernel,
        out_shape=(out, out),
        grid=(nblk,),
        in_specs=stream(rbs[0]) + stream(rbs[1]) + [_full_spec(w2pad), _full_spec(bias), _full_spec(wall),
                                                     _full_spec(pmask)],
        out_specs=tuple(pl.BlockSpec((SCAN_ROWS, GLA_W), lambda i, rb=rb: (rb(i), 0)) for rb in rbs),
        scratch_shapes=[pltpu.VMEM((2 * GLA_HEADS, HEAD, HEAD), F32), pltpu.VMEM((2, SCAN_ROWS, GLA_W), F32)],
        compiler_params=_cparams(("arbitrary",)),
        name="gla",
    )(z, z, z, zs, z, z, z, zs, w2pad, bias, wall, pmask)


def _rope_kernel(q_ref, k_ref, v_ref, cos_ref, sin_ref, qo_ref, ko_ref, vo_ref):
    w = q_ref.shape[1]
    cos = jnp.concatenate([cos_ref[...]] * DIFF_HEADS, axis=1)
    sin = jnp.concatenate([sin_ref[...]] * DIFF_HEADS, axis=1)
    lane = lax.broadcasted_iota(jnp.int32, q_ref.shape, 1)
    first = (lane % 32) < 16

    def rot(x):
        swapped = jnp.where(first, pltpu.roll(x, w - 16, axis=1), pltpu.roll(x, 16, axis=1))
        return x * cos + swapped * sin

    qo_ref[...] = _bf(rot(q_ref[...].astype(F32)) * (DIFF_DQK ** -0.5 * math.log2(math.e)))
    ko_ref[...] = _bf(rot(k_ref[...].astype(F32)))
    lane_h = lax.broadcasted_iota(jnp.int32, (q_ref.shape[0], HEAD), 1)
    ones_col = jnp.where(lane_h == 0, 1.0, 0.0).astype(BF16)
    for h in range(DIFF_HEADS):
        vo_ref[:, 2 * h * HEAD:(2 * h + 1) * HEAD] = v_ref[:, h * HEAD:(h + 1) * HEAD]
        vo_ref[:, (2 * h + 1) * HEAD:(2 * h + 2) * HEAD] = ones_col


def _rope(z, cos_t, sin_t):
    ntok = z.shape[0]
    tm = SCAN_ROWS
    spec = lambda col: pl.BlockSpec((tm, DIFF_W), lambda i: (i, col // DIFF_W))
    out = jax.ShapeDtypeStruct((ntok, DIFF_W), BF16)
    return pl.pallas_call(
        _rope_kernel,
        out_shape=(out, out, jax.ShapeDtypeStruct((ntok, 2 * DIFF_W), BF16)),
        grid=(ntok // tm,),
        in_specs=[spec(COL_DQ), spec(COL_DK), spec(COL_DV),
                  pl.BlockSpec((tm, HEAD), lambda i: (i, 0)),
                  pl.BlockSpec((tm, HEAD), lambda i: (i, 0))],
        out_specs=(pl.BlockSpec((tm, DIFF_W), lambda i: (i, 0)), pl.BlockSpec((tm, DIFF_W), lambda i: (i, 0)),
                   pl.BlockSpec((tm, 2 * DIFF_W), lambda i: (i, 0))),
        compiler_params=_cparams(("arbitrary",)),
        name="rope",
    )(z, z, z, cos_t, sin_t)


def _attn_kernel(lam_ref, q_ref, k_ref, v_ref, o_ref, *, kv_tiles_ctx, kv_tiles_all, lam_init):
    tq = q_ref.shape[0]
    q = q_ref[...]
    lane = lax.broadcasted_iota(jnp.int32, q.shape, 1)
    zero = jnp.zeros_like(q)
    q2 = jnp.concatenate([jnp.where(lane < DIFF_DQK, q, zero), jnp.where(lane >= DIFF_DQK, q, zero)], axis=0)
    lp = lam_ref[...]
    lam = (jnp.exp(jnp.sum(lp[0:1] * lp[1:2], axis=-1, keepdims=True))
           - jnp.exp(jnp.sum(lp[2:3] * lp[3:4], axis=-1, keepdims=True)) + lam_init)

    def attend(kv_tiles):
        m = jnp.full((2 * tq, 1), -jnp.inf, F32)
        acc = jnp.zeros((2 * tq, 2 * HEAD), F32)
        for r0, r1 in kv_tiles:
            s = _dot_nt(q2, k_ref[r0:r1, :])
            m_new = jnp.maximum(m, jnp.max(s, axis=-1, keepdims=True))
            acc = jnp.exp2(m - m_new) * acc + _dot(_bf(jnp.exp2(s - m_new)), v_ref[r0:r1, :])
            m = m_new
        o = acc[:, 0:HEAD] / acc[:, HEAD:HEAD + 1]
        o_ref[...] = _bf(o[0:tq] - lam * o[tq:2 * tq])

    is_ctx = pl.program_id(1) < CTX_LEN // tq

    @pl.when(is_ctx)
    def _():
        attend(kv_tiles_ctx)

    @pl.when(jnp.logical_not(is_ctx))
    def _():
        attend(kv_tiles_all)


def _kv_tiles(n, tile):
    return tuple((r, min(r + tile, n)) for r in range(0, n, tile))


def _attn(lam_p, q, k, v, kv_tile, lam_init):
    ntok = q.shape[0]
    tq = ATT_TQ
    return pl.pallas_call(
        functools.partial(_attn_kernel, kv_tiles_ctx=_kv_tiles(CTX_LEN, kv_tile), kv_tiles_all=_kv_tiles(ntok, kv_tile),
                          lam_init=lam_init),
        out_shape=jax.ShapeDtypeStruct((ntok, DIFF_W), BF16),
        grid=(DIFF_HEADS, ntok // tq),
        in_specs=[
            pl.BlockSpec(lam_p.shape, lambda h, i: (0, 0)),
            pl.BlockSpec((tq, HEAD), lambda h, i: (i, h)),
            pl.BlockSpec((ntok, HEAD), lambda h, i: (0, h)),
            pl.BlockSpec((ntok, 2 * HEAD), lambda h, i: (0, h)),
        ],
        out_specs=pl.BlockSpec((tq, HEAD), lambda h, i: (i, h)),
        compiler_params=_cparams(("arbitrary", "arbitrary")),
        name="diff_attn",
    )(lam_p, q, k, v)


def _conv_kernel(xp_ref, x_ref, xn_ref, w_ref, o_ref):
    i = pl.program_id(0)
    nblk = pl.num_programs(0)
    nctx = CTX_LEN // SCAN_ROWS
    tm = x_ref.shape[0]
    pad = DELTA_CONV // 2
    has_prev = jnp.logical_and(i != 0, i != nctx)
    has_next = jnp.logical_and(i != nctx - 1, i != nblk - 1)
    prev = jnp.where(has_prev, xp_ref[...].astype(F32), 0.0)
    nxt = jnp.where(has_next, xn_ref[...].astype(F32), 0.0)
    xe = jnp.concatenate([prev, x_ref[...].astype(F32), nxt], axis=0)
    acc = None
    for j in range(DELTA_CONV):
        off = CONV_HALO + j - pad
        term = xe[off:off + tm, :] * w_ref[j:j + 1, :]
        acc = term if acc is None else acc + term
    y = _silu(acc)
    jc = pl.program_id(1)
    q_blocks = DELTA_W // x_ref.shape[1]
    is_qk = jc < 2 * q_blocks
    post = jnp.where(jc < q_blocks, HEAD ** -0.5, 1.0)
    for hh in range(x_ref.shape[1] // HEAD):
        cols = slice(hh * HEAD, (hh + 1) * HEAD)
        yh = y[:, cols]
        yn = yh * lax.rsqrt(jnp.sum(yh * yh, axis=-1, keepdims=True) + L2_EPS) * post
        o_ref[:, cols] = _bf(jnp.where(is_qk, yn, yh))


def _conv(z, conv_w):
    ntok = z.shape[0]
    tm = SCAN_ROWS
    width = 3 * DELTA_W
    nbh = ntok // CONV_HALO
    rh = tm // CONV_HALO
    cw = DELTA_W
    ncb = width // cw
    c0 = COL_EQ // cw
    return pl.pallas_call(
        _conv_kernel,
        out_shape=jax.ShapeDtypeStruct((ntok, width), BF16),
        grid=(ntok // tm, ncb),
        in_specs=[
            pl.BlockSpec((CONV_HALO, cw), lambda i, j: (jnp.maximum(i * rh - 1, 0), c0 + j)),
            pl.BlockSpec((tm, cw), lambda i, j: (i, c0 + j)),
            pl.BlockSpec((CONV_HALO, cw), lambda i, j: (jnp.minimum((i + 1) * rh, nbh - 1), c0 + j)),
            pl.BlockSpec((DELTA_CONV, cw), lambda i, j: (0, j)),
        ],
        out_specs=pl.BlockSpec((tm, cw), lambda i, j: (i, j)),
        compiler_params=_cparams(("arbitrary", "arbitrary")),
        name="delta_conv",
    )(z, z, z, conv_w)


def _delta_consts(reverse):
    c = CHUNK
    i = np.arange(c)[:, None]
    t = np.arange(c)[None, :]
    tri = (t >= i) if reverse else (t <= i)
    incl = tri
    strict = (t > i) if reverse else (t < i)
    cc = np.arange(4 * c)[None, :]
    j = cc % c
    eye = (i == j)
    m16 = (i // 16) == (j // 16)
    q32 = ((i // 32) == (j // 32)) & ~m16
    q64 = (i // 32) != (j // 32)
    r = np.arange(4 * c)[:, None]
    bd = (r // c) == (cc // c)
    f = lambda a: jnp.asarray(a.astype(np.float32))
    return dict(tri=_rep3(tri, 1), trit=_rep3(tri.T, 0),
                incl=f(incl), strict=f(strict), eye=f(eye), m16=f(m16), q32=f(q32), q64=f(q64),
                bd=jnp.asarray(bd.astype(np.float32), BF16))


def _blockdiag(xcat, bd):
    return jnp.concatenate([xcat] * 4, axis=0) * bd


def _delta_kernel(qf_ref, kf_ref, vf_ref, smf_ref, smtf_ref, qb_ref, kb_ref, vb_ref, smb_ref, smtb_ref, pr_ref, pc_ref,
                  tri_ref, trit_ref, incl_ref, strict_ref, eye_ref, m16_ref, q32_ref, q64_ref, bd_ref,
                  of_ref, ob_ref, s_scr, g_scr, b_scr, gt_scr):
    @pl.when(pl.program_id(0) == 0)
    def _():
        s_scr[...] = jnp.zeros_like(s_scr)

    streams = ((qf_ref, kf_ref, vf_ref, smf_ref, smtf_ref, of_ref), (qb_ref, kb_ref, vb_ref, smb_ref, smtb_ref, ob_ref))
    n_chunks = qf_ref.shape[0] // CHUNK
    for d, (_, _, _, sm_ref, smt_ref, _) in enumerate(streams):
        sm = sm_ref[:, 0:HEAD]
        g_scr[d] = -jnp.exp(pr_ref[0:1, :]) * _softplus(sm + pr_ref[1:2, :])
        b_scr[d] = _sigmoid(sm)
        gt = -jnp.exp(pc_ref[:, 0:1]) * _softplus(smt_ref[...] + pc_ref[:, 1:2])
        for c in range(n_chunks):
            gt_scr[d, c] = _dot_exact_rhs(gt[SM_A:SM_A + 2 * DELTA_HEADS, c * CHUNK:(c + 1) * CHUNK], trit_ref[d])
    last = (CHUNK - 1, 0)
    bd = bd_ref[...]

    def mm(a, b):
        return _dot(_bf(a), _blockdiag(_bf(b), bd))

    local = {}
    units = []
    for d, (q_ref, k_ref, v_ref, _, _, _) in enumerate(streams):
        incl = incl_ref[d] > 0.0
        strict = strict_ref[d] > 0.0
        for cc in range(n_chunks):
            rows = slice(cc * CHUNK, (cc + 1) * CHUNK)
            gc = _dot_exact_lhs(tri_ref[d], g_scr[d, rows, :])
            beta = b_scr[d, rows, :]
            gct = gt_scr[d, cc]
            for grp in range(DELTA_HEADS // 4):
                a_list, att_list, rhs_list, misc = [], [], [], []
                for hh in range(4):
                    h = grp * 4 + hh
                    cols = slice(h * HEAD, (h + 1) * HEAD)
                    la = SM_A + DELTA_HEADS * d + h
                    lb = SM_B + DELTA_HEADS * d + h
                    gcol = gc[:, la:la + 1]
                    grow = gct[DELTA_HEADS * d + h:DELTA_HEADS * d + h + 1, :]
                    bcol = beta[:, lb:lb + 1]
                    q = q_ref[rows, cols].astype(F32)
                    k = k_ref[rows, cols].astype(F32)
                    v = v_ref[rows, cols].astype(F32)
                    dec = jnp.exp(jnp.where(incl, gcol - grow, -jnp.inf))
                    kb = k * bcol
                    kbf = _bf(k)
                    a_list.append(jnp.where(strict, _dot_nt(_bf(kb), kbf) * dec, 0.0))
                    att_list.append(_bf(_dot_nt(_bf(q), kbf) * dec))
                    eg = jnp.exp(gcol)
                    rhs_list.append(_bf(jnp.concatenate([v * bcol, kb * eg], axis=1)))
                    g_last = gcol[last[d]:last[d] + 1, :]
                    misc.append((_bf(q * eg), _bf(k * jnp.exp(g_last - gcol)), jnp.exp(g_last)))
                units.append(((d, cc, grp), jnp.concatenate(a_list, axis=1), att_list,
                              jnp.concatenate(rhs_list, axis=0), misc))

    acat = [u_[1] for u_ in units]
    n = [-(a * m16_ref[...]) for a in acat]
    p = [eye_ref[...] + n_ for n_ in n]
    for _ in range(3):
        n = [mm(n_, n_) for n_ in n]
        p = [p_ + mm(p_, n_) for p_, n_ in zip(p, n)]
    for q_ref_ in (q32_ref, q64_ref):
        x = [mm(p_, a * q_ref_[...]) for p_, a in zip(p, acat)]
        p = [p_ - mm(x_, p_) for p_, x_ in zip(p, x)]
    sol = [_dot(_blockdiag(_bf(p_), bd), u_[3]) for p_, u_ in zip(p, units)]
    for ((d, cc, grp), _, att_list, _, misc), sol_ in zip(units, sol):
        for hh in range(4):
            u = sol_[hh * CHUNK:(hh + 1) * CHUNK, 0:HEAD]
            w = _bf(sol_[hh * CHUNK:(hh + 1) * CHUNK, HEAD:2 * HEAD])
            local[(d, cc, grp * 4 + hh)] = (u, w, att_list[hh]) + misc[hh]

    s = [s_scr[i] for i in range(2 * DELTA_HEADS)]
    for ci in range(n_chunks):
        keys = [(d, ci if d == 0 else n_chunks - 1 - ci, h) for d in range(2) for h in range(DELTA_HEADS)]
        sb = [_bf(s_) for s_ in s]
        vnb = [_bf(local[key][0] - _dot(local[key][1], sb[i])) for i, key in enumerate(keys)]
        for i, key in enumerate(keys):
            d, cc, h = key
            _, _, att, qd, kd, egl = local[key]
            streams[d][5][cc * CHUNK:(cc + 1) * CHUNK, h * HEAD:(h + 1) * HEAD] = _bf(
                _dot(qd, sb[i]) + _dot(att, vnb[i]))
            s[i] = s[i] * egl + _dot_tn(kd, vnb[i])
    for i in range(2 * DELTA_HEADS):
        s_scr[i] = s[i]


def _delta(qkv, zs, zt, prow, pcol):
    ntok = zs.shape[0]
    nblk = ntok // SCAN_ROWS
    csts = [_delta_consts(rev) for rev in (False, True)]
    per_dir = [jnp.stack([c_[n] for c_ in csts]) for n in ("tri", "trit", "incl", "strict")]
    shared = [csts[0][n] for n in ("eye", "m16", "q32", "q64", "bd")]
    rbs = [functools.partial(_scan_block_index, nblk=nblk, reverse=rev) for rev in (False, True)]
    col = lambda rb, j: pl.BlockSpec((SCAN_ROWS, DELTA_W), lambda i: (rb(i), j))
    stream = lambda rb: [col(rb, 0), col(rb, 1), col(rb, 2), pl.BlockSpec((SCAN_ROWS, SM_W), lambda i: (rb(i), 0)),
                         pl.BlockSpec((HEAD, SCAN_ROWS), lambda i: (0, rb(i)))]
    consts = [prow, pcol] + per_dir + shared
    out = jax.ShapeDtypeStruct((ntok, DELTA_W), BF16)
    return pl.pallas_call(
        _delta_kernel,
        out_shape=(out, out),
        grid=(nblk,),
        in_specs=stream(rbs[0]) + stream(rbs[1]) + [_full_spec(a) for a in consts],
        out_specs=tuple(pl.BlockSpec((SCAN_ROWS, DELTA_W), lambda i, rb=rb: (rb(i), 0)) for rb in rbs),
        scratch_shapes=[pltpu.VMEM((2 * DELTA_HEADS, HEAD, HEAD), F32),
                        pltpu.VMEM((2, SCAN_ROWS, HEAD), F32),
                        pltpu.VMEM((2, SCAN_ROWS, HEAD), F32),
                        pltpu.VMEM((2, SCAN_ROWS // CHUNK, 2 * DELTA_HEADS, CHUNK), F32)],
        compiler_params=_cparams(("arbitrary",)),
        name="delta",
    )(qkv, qkv, qkv, zs, zt, qkv, qkv, qkv, zs, zt, *consts)


def _head_rms(x, w_ref, n_heads):
    parts = []
    for h in range(n_heads):
        xh = x[:, h * HEAD:(h + 1) * HEAD]
        parts.append(xh * lax.rsqrt(jnp.mean(xh * xh, axis=-1, keepdims=True) + RMS_EPS) * w_ref[...])
    return jnp.concatenate(parts, axis=1)


def _up_kernel(oaf_ref, oab_ref, od_ref, oef_ref, oeb_ref, gg_ref, eg_ref, ga_ref, gd_ref, ge_ref,
               nwa_ref, nwd_ref, nwe_ref, wa_ref, wd_ref, we_ref, y_ref, a_scr, d_scr, e_scr, *, lam_init):
    @pl.when(pl.program_id(1) == 0)
    def _():
        gg = gg_ref[...].astype(F32)
        eg = eg_ref[...].astype(F32)
        oa = oaf_ref[...].astype(F32) + oab_ref[...].astype(F32)
        oe = oef_ref[...].astype(F32) + oeb_ref[...].astype(F32)
        a_scr[...] = _bf(_head_rms(oa, nwa_ref, GLA_HEADS) * _silu(gg))
        d_scr[...] = _bf(_head_rms(od_ref[...].astype(F32), nwd_ref, DIFF_HEADS) * (1.0 - lam_init))
        e_scr[...] = _bf(_head_rms(oe, nwe_ref, DELTA_HEADS) * _silu(eg))

    y = _sigmoid(ga_ref[...].astype(F32)) * _dot(a_scr[...], wa_ref[...])
    y = y + _sigmoid(gd_ref[...].astype(F32)) * _dot(d_scr[...], wd_ref[...])
    y = y + _sigmoid(ge_ref[...].astype(F32)) * _dot(e_scr[...], we_ref[...])
    y_ref[...] = _bf(y)


def _up(oaf, oab, od, oef, oeb, z, nwa, nwd, nwe, wa, wd, we, layer, lam_init, tm):
    ntok = z.shape[0]
    tn = UP_TN
    rows = lambda w: pl.BlockSpec((tm, w), lambda i, j: (i, 0))
    zcol = lambda col, w: pl.BlockSpec((tm, w), lambda i, j: (i, col // w))
    gate = lambda b: pl.BlockSpec((tm, tn), lambda i, j: (i, (COL_MG + b * D_MODEL) // tn + j))
    vec = pl.BlockSpec((1, HEAD), lambda i, j: (0, 0))
    wspec = lambda k: pl.BlockSpec((None, k, tn), lambda i, j: (layer, 0, j))
    return pl.pallas_call(
        functools.partial(_up_kernel, lam_init=lam_init),
        out_shape=jax.ShapeDtypeStruct((ntok, D_MODEL), BF16),
        grid=(ntok // tm, D_MODEL // tn),
        in_specs=[rows(GLA_W), rows(GLA_W), rows(DIFF_W), rows(DELTA_W), rows(DELTA_W),
                  zcol(COL_GG, GLA_W), zcol(COL_EG, DELTA_W), gate(0), gate(1), gate(2),
                  vec, vec, vec, wspec(GLA_W), wspec(DIFF_W), wspec(DELTA_W)],
        out_specs=pl.BlockSpec((tm, tn), lambda i, j: (i, j)),
        scratch_shapes=[pltpu.VMEM((tm, GLA_W), BF16), pltpu.VMEM((tm, DIFF_W), BF16), pltpu.VMEM((tm, DELTA_W), BF16)],
        compiler_params=_cparams(("arbitrary", "arbitrary")),
        name="up_merge",
    )(oaf, oab, od, oef, oeb, z, z, z, z, z, nwa.reshape(1, HEAD), nwd.reshape(1, HEAD), nwe.reshape(1, HEAD),
      wa, wd, we)


def _oproj_kernel(y_ref, x_ref, wo_ref, nw_ref, gate_ref, o_ref, t_scr):
    t_scr[...] = _dot(y_ref[...], wo_ref[...])
    _gated_residual(x_ref, t_scr, nw_ref, gate_ref, o_ref, pl.program_id(0) * x_ref.shape[0])


def _oproj(y, xc, wo, norm_w, mods, layer, tm):
    ntok, d = xc.shape
    return pl.pallas_call(
        _oproj_kernel,
        out_shape=jax.ShapeDtypeStruct((ntok, d), F32),
        grid=(ntok // tm,),
        in_specs=[
            pl.BlockSpec((tm, d), lambda i: (i, 0)),
            pl.BlockSpec((tm, d), lambda i: (i, 0)),
            pl.BlockSpec((None, d, d), lambda i: (layer, 0, 0), pipeline_mode=pl.Buffered(1)),
            pl.BlockSpec((1, d), lambda i: (0, 0)),
            pl.BlockSpec((8, d), lambda i: (0, 2)),
        ],
        out_specs=pl.BlockSpec((tm, d), lambda i: (i, 0)),
        scratch_shapes=[pltpu.VMEM((tm, d), F32)],
        compiler_params=_cparams(("arbitrary",)),
        name="out_proj",
    )(y, xc, wo, norm_w.reshape(1, d), mods)


def _ffn_kernel(x_ref, nw_ref, sh_ref, sc_ref, gate_ref, pw_ref, w1_ref, w3_ref, w2_ref, o_ref, h_scr, acc_scr):
    j = pl.program_id(1)
    row0 = pl.program_id(0) * x_ref.shape[0]

    @pl.when(j == 0)
    def _():
        _norm_modulate(x_ref, nw_ref, sh_ref, sc_ref, h_scr, row0)
        acc_scr[...] = jnp.zeros_like(acc_scr)

    h = h_scr[...]
    u = _silu(_dot(h, w1_ref[...])) * _dot(h, w3_ref[...])
    acc_scr[...] += _dot(_bf(u), w2_ref[...])

    @pl.when(j == pl.num_programs(1) - 1)
    def _():
        _gated_residual(x_ref, acc_scr, pw_ref, gate_ref, o_ref, row0)


def _ffn(xc, pre_w, post_w, mods, w1, w3, w2, layer, tm):
    ntok, d = xc.shape
    dff = w1.shape[2]
    return pl.pallas_call(
        _ffn_kernel,
        out_shape=jax.ShapeDtypeStruct((ntok, d), F32),
        grid=(ntok // tm, dff // FF_TN),
        in_specs=[
            pl.BlockSpec((tm, d), lambda i, j: (i, 0), pipeline_mode=pl.Buffered(1)),
            pl.BlockSpec((1, d), lambda i, j: (0, 0)),
            pl.BlockSpec((8, d), lambda i, j: (0, 3)),
            pl.BlockSpec((8, d), lambda i, j: (0, 4)),
            pl.BlockSpec((8, d), lambda i, j: (0, 5)),
            pl.BlockSpec((1, d), lambda i, j: (0, 0)),
            pl.BlockSpec((None, d, FF_TN), lambda i, j: (layer, 0, j)),
            pl.BlockSpec((None, d, FF_TN), lambda i, j: (layer, 0, j)),
            pl.BlockSpec((None, FF_TN, d), lambda i, j: (layer, j, 0)),
        ],
        out_specs=pl.BlockSpec((tm, d), lambda i, j: (i, 0), pipeline_mode=pl.Buffered(1)),
        scratch_shapes=[pltpu.VMEM((tm, d), BF16), pltpu.VMEM((tm, d), F32)],
        compiler_params=_cparams(("arbitrary", "arbitrary")),
        name="ffn",
    )(xc, pre_w.reshape(1, d), mods, mods, mods, post_w.reshape(1, d), w1, w3, w2)


def _permute_w_in(w):
    main = [w[..., 0:1536], w[..., 1568:2080], w[..., 6720:7744], w[..., 3616:6688], w[..., 2080:3616],
            w[..., 7744:13888]]
    small = [w[..., 1536:1568], w[..., 6688:6720], jnp.zeros(w.shape[:2] + (SM_W - 64,), w.dtype)]
    return _bf(jnp.concatenate(main, axis=-1)), _bf(jnp.concatenate(small, axis=-1))


def _rope_tables(seq):
    rows = seq // GRID_W
    row_ids = jnp.repeat(jnp.arange(rows, dtype=F32), GRID_W)
    col_ids = jnp.tile(jnp.arange(GRID_W, dtype=F32), rows)
    half = DIFF_DQK // 2
    inv = 1.0 / (ROPE_BASE ** (jnp.arange(0, half, 2, dtype=F32) / half))
    ang_r = row_ids[:, None] * inv
    ang_c = col_ids[:, None] * inv
    cr, sr, cc, sc = jnp.cos(ang_r), jnp.sin(ang_r), jnp.cos(ang_c), jnp.sin(ang_c)
    cos_lat = jnp.tile(jnp.concatenate([cr, cr, cc, cc], axis=1), (1, 2))
    sin_lat = jnp.tile(jnp.concatenate([-sr, sr, -sc, sc], axis=1), (1, 2))
    cos_t = jnp.concatenate([jnp.ones((CTX_LEN, HEAD), F32), cos_lat], axis=0)
    sin_t = jnp.concatenate([jnp.zeros((CTX_LEN, HEAD), F32), sin_lat], axis=0)
    return cos_t, sin_t


def _lane_params(a_log, dt_bias):
    flat_a = a_log.reshape(-1)
    flat_b = dt_bias.reshape(-1)
    n = flat_a.shape[0]
    pa = jnp.zeros((HEAD,), F32).at[SM_A:SM_A + n].set(flat_a)
    pb = jnp.zeros((HEAD,), F32).at[SM_A:SM_A + n].set(flat_b)
    prow = jnp.zeros((8, HEAD), F32).at[0].set(pa).at[1].set(pb)
    pcol = jnp.zeros((HEAD, HEAD), F32).at[:, 0].set(pa).at[:, 1].set(pb)
    return prow, pcol


def _gla_gate_weights(w2, bias):
    wpad = jnp.zeros((2, HEAD, GLA_W), F32)
    for dirn in range(2):
        wpad = wpad.at[dirn, GLA_GATE_RANK * dirn:GLA_GATE_RANK * (dirn + 1)].set(w2[dirn])
    return _bf(wpad), bias.reshape(2, 1, GLA_W)


def kernel(x, c, ctx, c_ctx, ada_w, ada_b, mix_pre_w, mix_post_w, ffn_pre_w, ffn_post_w, w_in, gla_gate_w2, gla_gate_b, gla_norm_w, diff_lambda, diff_norm_w, delta_conv_w, delta_a_log, delta_dt_bias, delta_norm_w, w_up_gla, w_up_diff, w_up_delta, w_o, ffn_w1, ffn_w3, ffn_w2):
    seq = x.shape[1]
    ntok = CTX_LEN + seq
    tm = ROW_TILE if ntok % ROW_TILE == 0 else SCAN_ROWS
    in_tm = IN_TM if ntok % IN_TM == 0 else SCAN_ROWS
    xc = jnp.concatenate([ctx[0], x[0]], axis=0)
    cond = jnp.concatenate([c, c_ctx[None, :], jnp.zeros((COND_ROWS - 2, D_MODEL), F32)], axis=0)
    mods_all = _ada(_bf(jax.nn.silu(cond)), ada_w, ada_b)
    cos_t, sin_t = _rope_tables(seq)
    kv_tile = ROW_TILE if ntok % ROW_TILE == 0 else SCAN_ROWS
    w_main, w_small = _permute_w_in(w_in)
    wa, wd, we, wo = _bf(w_up_gla), _bf(w_up_diff), _bf(w_up_delta), _bf(w_o)
    w1, w3, w2 = _bf(ffn_w1), _bf(ffn_w3), _bf(ffn_w2)

    for layer in range(DEPTH):
        lam_init = 0.8 - 0.6 * math.exp(-0.3 * layer)
        mods = mods_all[layer]
        z, zs = _inproj(xc, mix_pre_w[layer], mods, w_main, w_small, layer, in_tm)

        oa = _gla(z, zs, *_gla_gate_weights(gla_gate_w2[layer], gla_gate_b[layer]))

        qr, kr, vr = _rope(z, cos_t, sin_t)
        od = _attn(diff_lambda[layer], qr, kr, vr, kv_tile, lam_init)

        qkv = _conv(z, delta_conv_w[layer])
        zt = zs[:, 0:HEAD].T
        prow, pcol = _lane_params(delta_a_log[layer], delta_dt_bias[layer])
        oe = _delta(qkv, zs, zt, prow, pcol)

        y = _up(oa[0], oa[1], od, oe[0], oe[1], z, gla_norm_w[layer], diff_norm_w[layer], delta_norm_w[layer],
                wa, wd, we, layer, lam_init, tm)
        xc = _oproj(y, xc, wo, mix_post_w[layer], mods, layer, tm)
        xc = _ffn(xc, ffn_pre_w[layer], ffn_post_w[layer], mods, w1, w3, w2, layer, in_tm)
    return xc[CTX_LEN:][None]
```

```python
import functools
import math

import numpy as np
import jax
import jax.numpy as jnp
from jax import lax
from jax.experimental import pallas as pl
from jax.experimental.pallas import tpu as pltpu

F32 = jnp.float32
BF16 = jnp.bfloat16

D_MODEL = 2048
DEPTH = 2
GRID_W = 64
CTX_LEN = 256
HEAD = 128
GLA_HEADS = 4
GLA_GATE_RANK = 16
GLA_GATE_NORM = 16.0
DIFF_HEADS = 4
DIFF_DQK = 64
DELTA_HEADS = 8
DELTA_CONV = 5
CHUNK = 64
ROPE_BASE = 10000.0
RMS_EPS = 1e-6
L2_EPS = 1e-6
D_FF = 5632
GLA_W = GLA_HEADS * HEAD
DIFF_W = DIFF_HEADS * HEAD
DELTA_W = DELTA_HEADS * HEAD

COL_GQ, COL_GK, COL_GV, COL_GG = 0, 512, 1024, 1536
COL_EG = 2048
COL_EQ, COL_EK, COL_EV = 3072, 4096, 5120
COL_DQ, COL_DK, COL_DV = 6144, 6656, 7168
COL_MG = 7680
NZ = 13824
SM_W = 256
SM_A = 32
SM_B = 48

ROW_TILE = 768
IN_TM = 1056
SCAN_ROWS = 256
IN_TN = 1536
FF_TN = 512
UP_TN = 512
ATT_TQ = 256
COND_ROWS = 16
CONV_HALO = 16
GLA_GROUP = 32
VMEM_LIMIT = 56 * 1024 * 1024


def _cparams(sem, vmem=VMEM_LIMIT):
    return pltpu.CompilerParams(dimension_semantics=sem, vmem_limit_bytes=vmem)


def _bf(x):
    return x.astype(BF16)


def _dot(a, b):
    return jnp.dot(a, b, preferred_element_type=F32)


def _dot_nt(a, b):
    return lax.dot_general(a, b, (((1,), (1,)), ((), ())), preferred_element_type=F32)


def _dot_tn(a, b):
    return lax.dot_general(a, b, (((0,), (0,)), ((), ())), preferred_element_type=F32)


def _split3(x):
    hi = _bf(x)
    r = x - hi.astype(F32)
    mid = _bf(r)
    lo = _bf(r - mid.astype(F32))
    return hi, mid, lo


def _dot_exact_lhs(m3_bf, x):
    return _dot(m3_bf, jnp.concatenate(_split3(x), axis=0))


def _dot_exact_rhs(x, m3_bf):
    return _dot(jnp.concatenate(_split3(x), axis=1), m3_bf)


def _rep3(m, axis):
    return jnp.asarray(np.concatenate([m.astype(np.float32)] * 3, axis=axis), BF16)


def _sigmoid(x):
    return 1.0 / (1.0 + jnp.exp(-x))


def _silu(x):
    return x * _sigmoid(x)


def _softplus(x):
    return jnp.maximum(x, 0.0) + jnp.log1p(jnp.exp(-jnp.abs(x)))


def _log_sigmoid(x):
    return jnp.minimum(x, 0.0) - jnp.log1p(jnp.exp(-jnp.abs(x)))


def _rms(x, w):
    return x * lax.rsqrt(jnp.mean(x * x, axis=-1, keepdims=True) + RMS_EPS) * w


ROW_CHUNK = 16
ROW_UNROLL = 6


def _mod_row(mod_ref, row0):
    return mod_ref[pl.ds((row0 < CTX_LEN).astype(jnp.int32), 1), :]


def _norm_modulate(x_ref, nw_ref, sh_ref, sc_ref, h_ref, row0):
    def body(r, carry):
        rows = pl.ds(pl.multiple_of(r * ROW_CHUNK, ROW_CHUNK), ROW_CHUNK)
        g0 = row0 + r * ROW_CHUNK
        h = _rms(x_ref[rows, :], nw_ref[...])
        h_ref[rows, :] = _bf(h * (1.0 + _mod_row(sc_ref, g0)) + _mod_row(sh_ref, g0))
        return carry

    lax.fori_loop(0, x_ref.shape[0] // ROW_CHUNK, body, 0, unroll=ROW_UNROLL)


def _gated_residual(x_ref, t_ref, nw_ref, gate_ref, o_ref, row0):
    def body(r, carry):
        rows = pl.ds(pl.multiple_of(r * ROW_CHUNK, ROW_CHUNK), ROW_CHUNK)
        g0 = row0 + r * ROW_CHUNK
        o_ref[rows, :] = x_ref[rows, :] + _mod_row(gate_ref, g0) * _rms(t_ref[rows, :], nw_ref[...])
        return carry

    lax.fori_loop(0, x_ref.shape[0] // ROW_CHUNK, body, 0, unroll=ROW_UNROLL)


def _ada_kernel(s_ref, w_ref, b_ref, o_ref):
    o_ref[0] = _dot(s_ref[...], _bf(w_ref[0])) + b_ref[0]


def _ada(s_bf, ada_w, ada_b):
    depth, d, n6 = ada_w.shape
    tn = 1024
    return pl.pallas_call(
        _ada_kernel,
        out_shape=jax.ShapeDtypeStruct((depth, COND_ROWS, n6), F32),
        grid=(depth, n6 // tn),
        in_specs=[
            pl.BlockSpec((COND_ROWS, d), lambda l, j: (0, 0)),
            pl.BlockSpec((1, d, tn), lambda l, j: (l, 0, j)),
            pl.BlockSpec((1, 1, tn), lambda l, j: (l, 0, j)),
        ],
        out_specs=pl.BlockSpec((1, COND_ROWS, tn), lambda l, j: (l, 0, j)),
        compiler_params=_cparams(("arbitrary", "arbitrary")),
        name="ada",
    )(s_bf, ada_w, ada_b.reshape(depth, 1, n6))


def _inproj_kernel(x_ref, nw_ref, sh_ref, sc_ref, w_ref, ws_ref, o_ref, os_ref, h_scr):
    @pl.when(pl.program_id(1) == 0)
    def _():
        _norm_modulate(x_ref, nw_ref, sh_ref, sc_ref, h_scr, pl.program_id(0) * x_ref.shape[0])
        os_ref[...] = _dot(h_scr[...], ws_ref[...])

    o_ref[...] = _bf(_dot(h_scr[...], w_ref[...]))


def _inproj(xc, norm_w, mods, w_bf, ws_bf, layer, tm):
    ntok, d = xc.shape
    nz = w_bf.shape[2]
    return pl.pallas_call(
        _inproj_kernel,
        out_shape=(jax.ShapeDtypeStruct((ntok, nz), BF16), jax.ShapeDtypeStruct((ntok, SM_W), F32)),
        grid=(ntok // tm, nz // IN_TN),
        in_specs=[
            pl.BlockSpec((tm, d), lambda i, j: (i, 0)),
            pl.BlockSpec((1, d), lambda i, j: (0, 0)),
            pl.BlockSpec((8, d), lambda i, j: (0, 0)),
            pl.BlockSpec((8, d), lambda i, j: (0, 1)),
            pl.BlockSpec((None, d, IN_TN), lambda i, j: (layer, 0, j)),
            pl.BlockSpec((None, d, SM_W), lambda i, j: (layer, 0, 0)),
        ],
        out_specs=(pl.BlockSpec((tm, IN_TN), lambda i, j: (i, j)), pl.BlockSpec((tm, SM_W), lambda i, j: (i, 0))),
        scratch_shapes=[pltpu.VMEM((tm, d), BF16)],
        compiler_params=_cparams(("arbitrary", "arbitrary")),
        name="inproj",
    )(xc, norm_w.reshape(1, d), mods, mods, w_bf, ws_bf)


def _gla_consts(reverse):
    c = CHUNK
    i = np.arange(c)[:, None]
    t = np.arange(c)[None, :]
    if not reverse:
        mats = [t <= i, t > i]
    else:
        mats = [t >= i, t < i]
    masks = [i == t]
    for m in (32, 16, 8, 4, 2, 1):
        p = (i // (2 * m)) * (2 * m) + m - 1
        second = ((i // m) % 2) == 1
        same = (i // (2 * m)) == (t // (2 * m))
        t_second = ((t // m) % 2) == 1
        if not reverse:
            w = np.where(second, (t > p) & (t <= i), (t > i) & (t <= p))
            pm = same & second & ~t_second
        else:
            w = np.where(second, (t >= p + 1) & (t <= i - 1), (t >= i) & (t <= p))
            pm = same & ~second & t_second
        mats.append(w)
        masks.append(pm)
    wall = np.concatenate([m_.astype(np.float32) for m_ in mats], axis=0)
    pmask = np.stack([m_.astype(np.float32) for m_ in masks], axis=0)
    return _rep3(wall, 1), jnp.asarray(pmask, F32)


def _gla_kernel(qf_ref, kf_ref, vf_ref, smf_ref, qb_ref, kb_ref, vb_ref, smb_ref, w2_ref, b_ref, wall_ref, pm_ref,
                of_ref, ob_ref, st_scr, g_scr):
    @pl.when(pl.program_id(0) == 0)
    def _():
        st_scr[...] = jnp.zeros_like(st_scr)

    streams = ((qf_ref, kf_ref, vf_ref, smf_ref, of_ref), (qb_ref, kb_ref, vb_ref, smb_ref, ob_ref))
    n_chunks = qf_ref.shape[0] // CHUNK
    scale = HEAD ** -0.5
    for d, (_, _, _, sm_ref, _) in enumerate(streams):
        x = _dot(_bf(sm_ref[:, 0:HEAD]), w2_ref[d]) + b_ref[d]
        g_scr[d] = _log_sigmoid(x) * (1.0 / GLA_GATE_NORM)

    units = []
    for d, (q_ref, k_ref, v_ref, _, _) in enumerate(streams):
        for cc in range(n_chunks):
            rows = slice(cc * CHUNK, (cc + 1) * CHUNK)
            e_all = jnp.exp(_dot_exact_lhs(wall_ref[d], g_scr[d, rows, :]))
            for h in range(GLA_HEADS):
                cols = slice(h * HEAD, (h + 1) * HEAD)
                units.append(((d, cc, h), q_ref[rows, cols].astype(F32) * scale, k_ref[rows, cols].astype(F32),
                              v_ref[rows, cols], e_all[:, cols]))
    o_intra, q_in, k_st, v_bf, e_last = {}, {}, {}, {}, {}
    last = (CHUNK - 1, 0)
    for g0 in range(0, len(units), GLA_GROUP):
        grp = units[g0:g0 + GLA_GROUP]
        att = [jnp.where(pm_ref[key[0], 0] > 0.0, _dot_nt(_bf(q), _bf(k)), 0.0) for (key, q, k, _, _) in grp]
        for lv in range(6):
            lvl = slice((2 + lv) * CHUNK, (3 + lv) * CHUNK)
            att = [a + jnp.where(pm_ref[key[0], lv + 1] > 0.0, _dot_nt(_bf(q * e[lvl]), _bf(k * e[lvl])), 0.0)
                   for a, (key, q, k, _, e) in zip(att, grp)]
        for a, (key, q, k, vb, e) in zip(att, grp):
            o_intra[key] = _dot(_bf(a), vb)
            q_in[key] = _bf(q * e[0:CHUNK])
            k_st[key] = _bf(k * e[CHUNK:2 * CHUNK])
            v_bf[key] = vb
            e_last[key] = e[last[key[0]]:last[key[0]] + 1, :]

    st = [st_scr[i] for i in range(2 * GLA_HEADS)]
    for ci in range(n_chunks):
        for d in range(2):
            cc = ci if d == 0 else n_chunks - 1 - ci
            o_ref = streams[d][4]
            for h in range(GLA_HEADS):
                key, si = (d, cc, h), d * GLA_HEADS + h
                o_ref[cc * CHUNK:(cc + 1) * CHUNK, h * HEAD:(h + 1) * HEAD] = _bf(
                    _dot_nt(q_in[key], _bf(st[si])) + o_intra[key])
                st[si] = st[si] * e_last[key] + _dot_tn(v_bf[key], k_st[key])
    for i in range(2 * GLA_HEADS):
        st_scr[i] = st[i]


def _scan_block_index(i, nblk, reverse):
    if not reverse:
        return i
    nctx = CTX_LEN // SCAN_ROWS
    return jnp.where(i < nctx, nctx - 1 - i, nblk - 1 + nctx - i)


def _full_spec(a):
    return pl.BlockSpec(a.shape, lambda i: (0,) * a.ndim)


def _gla(z, zs, w2pad, bias):
    ntok = z.shape[0]
    nblk = ntok // SCAN_ROWS
    consts = [_gla_consts(rev) for rev in (False, True)]
    wall = jnp.stack([c_[0] for c_ in consts])
    pmask = jnp.stack([c_[1] for c_ in consts])
    rbs = [functools.partial(_scan_block_index, nblk=nblk, reverse=rev) for rev in (False, True)]
    zspec = lambda rb, col: pl.BlockSpec((SCAN_ROWS, GLA_W), lambda i: (rb(i), col // GLA_W))
    stream = lambda rb: [zspec(rb, COL_GQ), zspec(rb, COL_GK), zspec(rb, COL_GV),
                         pl.BlockSpec((SCAN_ROWS, SM_W), lambda i: (rb(i), 0))]
    out = jax.ShapeDtypeStruct((ntok, GLA_W), BF16)
    return pl.pallas_call(
        _gla_kernel,
        out_shape=(out, out),
        grid=(nblk,),
        in_specs=stream(rbs[0]) + stream(rbs[1]) + [_full_spec(w2pad), _full_spec(bias), _full_spec(wall),
                                                     _full_spec(pmask)],
        out_specs=tuple(pl.BlockSpec((SCAN_ROWS, GLA_W), lambda i, rb=rb: (rb(i), 0)) for rb in rbs),
        scratch_shapes=[pltpu.VMEM((2 * GLA_HEADS, HEAD, HEAD), F32), pltpu.VMEM((2, SCAN_ROWS, GLA_W), F32)],
        compiler_params=_cparams(("arbitrary",)),
        name="gla",
    )(z, z, z, zs, z, z, z, zs, w2pad, bias, wall, pmask)


def _rope_kernel(q_ref, k_ref, v_ref, cos_ref, sin_ref, qo_ref, ko_ref, vo_ref):
    w = q_ref.shape[1]
    cos = jnp.concatenate([cos_ref[...]] * DIFF_HEADS, axis=1)
    sin = jnp.concatenate([sin_ref[...]] * DIFF_HEADS, axis=1)
    lane = lax.broadcasted_iota(jnp.int32, q_ref.shape, 1)
    first = (lane % 32) < 16

    def rot(x):
        swapped = jnp.where(first, pltpu.roll(x, w - 16, axis=1), pltpu.roll(x, 16, axis=1))
        return x * cos + swapped * sin

    qo_ref[...] = _bf(rot(q_ref[...].astype(F32)) * (DIFF_DQK ** -0.5 * math.log2(math.e)))
    ko_ref[...] = _bf(rot(k_ref[...].astype(F32)))
    lane_h = lax.broadcasted_iota(jnp.int32, (q_ref.shape[0], HEAD), 1)
    ones_col = jnp.where(lane_h == 0, 1.0, 0.0).astype(BF16)
    for h in range(DIFF_HEADS):
        vo_ref[:, 2 * h * HEAD:(2 * h + 1) * HEAD] = v_ref[:, h * HEAD:(h + 1) * HEAD]
        vo_ref[:, (2 * h + 1) * HEAD:(2 * h + 2) * HEAD] = ones_col


def _rope(z, cos_t, sin_t):
    ntok = z.shape[0]
    tm = SCAN_ROWS
    spec = lambda col: pl.BlockSpec((tm, DIFF_W), lambda i: (i, col // DIFF_W))
    out = jax.ShapeDtypeStruct((ntok, DIFF_W), BF16)
    return pl.pallas_call(
        _rope_kernel,
        out_shape=(out, out, jax.ShapeDtypeStruct((ntok, 2 * DIFF_W), BF16)),
        grid=(ntok // tm,),
        in_specs=[spec(COL_DQ), spec(COL_DK), spec(COL_DV),
                  pl.BlockSpec((tm, HEAD), lambda i: (i, 0)),
                  pl.BlockSpec((tm, HEAD), lambda i: (i, 0))],
        out_specs=(pl.BlockSpec((tm, DIFF_W), lambda i: (i, 0)), pl.BlockSpec((tm, DIFF_W), lambda i: (i, 0)),
                   pl.BlockSpec((tm, 2 * DIFF_W), lambda i: (i, 0))),
        compiler_params=_cparams(("arbitrary",)),
        name="rope",
    )(z, z, z, cos_t, sin_t)


def _attn_kernel(lam_ref, q_ref, k_ref, v_ref, o_ref, *, kv_tiles_ctx, kv_tiles_all, lam_init):
    tq = q_ref.shape[0]
    q = q_ref[...]
    lane = lax.broadcasted_iota(jnp.int32, q.shape, 1)
    zero = jnp.zeros_like(q)
    q2 = jnp.concatenate([jnp.where(lane < DIFF_DQK, q, zero), jnp.where(lane >= DIFF_DQK, q, zero)], axis=0)
    lp = lam_ref[...]
    lam = (jnp.exp(jnp.sum(lp[0:1] * lp[1:2], axis=-1, keepdims=True))
           - jnp.exp(jnp.sum(lp[2:3] * lp[3:4], axis=-1, keepdims=True)) + lam_init)

    def attend(kv_tiles):
        m = jnp.full((2 * tq, 1), -jnp.inf, F32)
        acc = jnp.zeros((2 * tq, 2 * HEAD), F32)
        for r0, r1 in kv_tiles:
            s = _dot_nt(q2, k_ref[r0:r1, :])
            m_new = jnp.maximum(m, jnp.max(s, axis=-1, keepdims=True))
            acc = jnp.exp2(m - m_new) * acc + _dot(_bf(jnp.exp2(s - m_new)), v_ref[r0:r1, :])
            m = m_new
        o = acc[:, 0:HEAD] / acc[:, HEAD:HEAD + 1]
        o_ref[...] = _bf(o[0:tq] - lam * o[tq:2 * tq])

    is_ctx = pl.program_id(1) < CTX_LEN // tq

    @pl.when(is_ctx)
    def _():
        attend(kv_tiles_ctx)

    @pl.when(jnp.logical_not(is_ctx))
    def _():
        attend(kv_tiles_all)


def _kv_tiles(n, tile):
    return tuple((r, min(r + tile, n)) for r in range(0, n, tile))


def _attn(lam_p, q, k, v, kv_tile, lam_init):
    ntok = q.shape[0]
    tq = ATT_TQ
    return pl.pallas_call(
        functools.partial(_attn_kernel, kv_tiles_ctx=_kv_tiles(CTX_LEN, kv_tile), kv_tiles_all=_kv_tiles(ntok, kv_tile),
                          lam_init=lam_init),
        out_shape=jax.ShapeDtypeStruct((ntok, DIFF_W), BF16),
        grid=(DIFF_HEADS, ntok // tq),
        in_specs=[
            pl.BlockSpec(lam_p.shape, lambda h, i: (0, 0)),
            pl.BlockSpec((tq, HEAD), lambda h, i: (i, h)),
            pl.BlockSpec((ntok, HEAD), lambda h, i: (0, h)),
            pl.BlockSpec((ntok, 2 * HEAD), lambda h, i: (0, h)),
        ],
        out_specs=pl.BlockSpec((tq, HEAD), lambda h, i: (i, h)),
        compiler_params=_cparams(("arbitrary", "arbitrary")),
        name="diff_attn",
    )(lam_p, q, k, v)


def _conv_kernel(xp_ref, x_ref, xn_ref, w_ref, o_ref):
    i = pl.program_id(0)
    nblk = pl.num_programs(0)
    nctx = CTX_LEN // SCAN_ROWS
    tm = x_ref.shape[0]
    pad = DELTA_CONV // 2
    has_prev = jnp.logical_and(i != 0, i != nctx)
    has_next = jnp.logical_and(i != nctx - 1, i != nblk - 1)
    prev = jnp.where(has_prev, xp_ref[...].astype(F32), 0.0)
    nxt = jnp.where(has_next, xn_ref[...].astype(F32), 0.0)
    xe = jnp.concatenate([prev, x_ref[...].astype(F32), nxt], axis=0)
    acc = None
    for j in range(DELTA_CONV):
        off = CONV_HALO + j - pad
        term = xe[off:off + tm, :] * w_ref[j:j + 1, :]
        acc = term if acc is None else acc + term
    y = _silu(acc)
    jc = pl.program_id(1)
    q_blocks = DELTA_W // x_ref.shape[1]
    is_qk = jc < 2 * q_blocks
    post = jnp.where(jc < q_blocks, HEAD ** -0.5, 1.0)
    for hh in range(x_ref.shape[1] // HEAD):
        cols = slice(hh * HEAD, (hh + 1) * HEAD)
        yh = y[:, cols]
        yn = yh * lax.rsqrt(jnp.sum(yh * yh, axis=-1, keepdims=True) + L2_EPS) * post
        o_ref[:, cols] = _bf(jnp.where(is_qk, yn, yh))


def _conv(z, conv_w):
    ntok = z.shape[0]
    tm = SCAN_ROWS
    width = 3 * DELTA_W
    nbh = ntok // CONV_HALO
    rh = tm // CONV_HALO
    cw = DELTA_W
    ncb = width // cw
    c0 = COL_EQ // cw
    return pl.pallas_call(
        _conv_kernel,
        out_shape=jax.ShapeDtypeStruct((ntok, width), BF16),
        grid=(ntok // tm, ncb),
        in_specs=[
            pl.BlockSpec((CONV_HALO, cw), lambda i, j: (jnp.maximum(i * rh - 1, 0), c0 + j)),
            pl.BlockSpec((tm, cw), lambda i, j: (i, c0 + j)),
            pl.BlockSpec((CONV_HALO, cw), lambda i, j: (jnp.minimum((i + 1) * rh, nbh - 1), c0 + j)),
            pl.BlockSpec((DELTA_CONV, cw), lambda i, j: (0, j)),
        ],
        out_specs=pl.BlockSpec((tm, cw), lambda i, j: (i, j)),
        compiler_params=_cparams(("arbitrary", "arbitrary")),
        name="delta_conv",
    )(z, z, z, conv_w)


def _delta_consts(reverse):
    c = CHUNK
    i = np.arange(c)[:, None]
    t = np.arange(c)[None, :]
    tri = (t >= i) if reverse else (t <= i)
    incl = tri
    strict = (t > i) if reverse else (t < i)
    cc = np.arange(4 * c)[None, :]
    j = cc % c
    eye = (i == j)
    m16 = (i // 16) == (j // 16)
    q32 = ((i // 32) == (j // 32)) & ~m16
    q64 = (i // 32) != (j // 32)
    r = np.arange(4 * c)[:, None]
    bd = (r // c) == (cc // c)
    f = lambda a: jnp.asarray(a.astype(np.float32))
    return dict(tri=_rep3(tri, 1), trit=_rep3(tri.T, 0),
                incl=f(incl), strict=f(strict), eye=f(eye), m16=f(m16), q32=f(q32), q64=f(q64),
                bd=jnp.asarray(bd.astype(np.float32), BF16))


def _blockdiag(xcat, bd):
    return jnp.concatenate([xcat] * 4, axis=0) * bd


def _delta_kernel(qf_ref, kf_ref, vf_ref, smf_ref, smtf_ref, qb_ref, kb_ref, vb_ref, smb_ref, smtb_ref, pr_ref, pc_ref,
                  tri_ref, trit_ref, incl_ref, strict_ref, eye_ref, m16_ref, q32_ref, q64_ref, bd_ref,
                  of_ref, ob_ref, s_scr, g_scr, b_scr, gt_scr):
    @pl.when(pl.program_id(0) == 0)
    def _():
        s_scr[...] = jnp.zeros_like(s_scr)

    streams = ((qf_ref, kf_ref, vf_ref, smf_ref, smtf_ref, of_ref), (qb_ref, kb_ref, vb_ref, smb_ref, smtb_ref, ob_ref))
    n_chunks = qf_ref.shape[0] // CHUNK
    for d, (_, _, _, sm_ref, smt_ref, _) in enumerate(streams):
        sm = sm_ref[:, 0:HEAD]
        g_scr[d] = -jnp.exp(pr_ref[0:1, :]) * _softplus(sm + pr_ref[1:2, :])
        b_scr[d] = _sigmoid(sm)
        gt = -jnp.exp(pc_ref[:, 0:1]) * _softplus(smt_ref[...] + pc_ref[:, 1:2])
        for c in range(n_chunks):
            gt_scr[d, c] = _dot_exact_rhs(gt[SM_A:SM_A + 2 * DELTA_HEADS, c * CHUNK:(c + 1) * CHUNK], trit_ref[d])
    last = (CHUNK - 1, 0)
    bd = bd_ref[...]

    def mm(a, b):
        return _dot(_bf(a), _blockdiag(_bf(b), bd))

    local = {}
    units = []
    for d, (q_ref, k_ref, v_ref, _, _, _) in enumerate(streams):
        incl = incl_ref[d] > 0.0
        strict = strict_ref[d] > 0.0
        for cc in range(n_chunks):
            rows = slice(cc * CHUNK, (cc + 1) * CHUNK)
            gc = _dot_exact_lhs(tri_ref[d], g_scr[d, rows, :])
            beta = b_scr[d, rows, :]
            gct = gt_scr[d, cc]
            for grp in range(DELTA_HEADS // 4):
                a_list, att_list, rhs_list, misc = [], [], [], []
                for hh in range(4):
                    h = grp * 4 + hh
                    cols = slice(h * HEAD, (h + 1) * HEAD)
                    la = SM_A + DELTA_HEADS * d + h
                    lb = SM_B + DELTA_HEADS * d + h
                    gcol = gc[:, la:la + 1]
                    grow = gct[DELTA_HEADS * d + h:DELTA_HEADS * d + h + 1, :]
                    bcol = beta[:, lb:lb + 1]
                    q = q_ref[rows, cols].astype(F32)
                    k = k_ref[rows, cols].astype(F32)
                    v = v_ref[rows, cols].astype(F32)
                    dec = jnp.exp(jnp.where(incl, gcol - grow, -jnp.inf))
                    kb = k * bcol
                    kbf = _bf(k)
                    a_list.append(jnp.where(strict, _dot_nt(_bf(kb), kbf) * dec, 0.0))
                    att_list.append(_bf(_dot_nt(_bf(q), kbf) * dec))
                    eg = jnp.exp(gcol)
                    rhs_list.append(_bf(jnp.concatenate([v * bcol, kb * eg], axis=1)))
                    g_last = gcol[last[d]:last[d] + 1, :]
                    misc.append((_bf(q * eg), _bf(k * jnp.exp(g_last - gcol)), jnp.exp(g_last)))
                units.append(((d, cc, grp), jnp.concatenate(a_list, axis=1), att_list,
                              jnp.concatenate(rhs_list, axis=0), misc))

    acat = [u_[1] for u_ in units]
    n = [-(a * m16_ref[...]) for a in acat]
    p = [eye_ref[...] + n_ for n_ in n]
    for _ in range(3):
        n = [mm(n_, n_) for n_ in n]
        p = [p_ + mm(p_, n_) for p_, n_ in zip(p, n)]
    for q_ref_ in (q32_ref, q64_ref):
        x = [mm(p_, a * q_ref_[...]) for p_, a in zip(p, acat)]
        p = [p_ - mm(x_, p_) for p_, x_ in zip(p, x)]
    sol = [_dot(_blockdiag(_bf(p_), bd), u_[3]) for p_, u_ in zip(p, units)]
    for ((d, cc, grp), _, att_list, _, misc), sol_ in zip(units, sol):
        for hh in range(4):
            u = sol_[hh * CHUNK:(hh + 1) * CHUNK, 0:HEAD]
            w = _bf(sol_[hh * CHUNK:(hh + 1) * CHUNK, HEAD:2 * HEAD])
            local[(d, cc, grp * 4 + hh)] = (u, w, att_list[hh]) + misc[hh]

    s = [s_scr[i] for i in range(2 * DELTA_HEADS)]
    for ci in range(n_chunks):
        keys = [(d, ci if d == 0 else n_chunks - 1 - ci, h) for d in range(2) for h in range(DELTA_HEADS)]
        sb = [_bf(s_) for s_ in s]
        vnb = [_bf(local[key][0] - _dot(local[key][1], sb[i])) for i, key in enumerate(keys)]
        for i, key in enumerate(keys):
            d, cc, h = key
            _, _, att, qd, kd, egl = local[key]
            streams[d][5][cc * CHUNK:(cc + 1) * CHUNK, h * HEAD:(h + 1) * HEAD] = _bf(
                _dot(qd, sb[i]) + _dot(att, vnb[i]))
            s[i] = s[i] * egl + _dot_tn(kd, vnb[i])
    for i in range(2 * DELTA_HEADS):
        s_scr[i] = s[i]


def _delta(qkv, zs, zt, prow, pcol):
    ntok = zs.shape[0]
    nblk = ntok // SCAN_ROWS
    csts = [_delta_consts(rev) for rev in (False, True)]
    per_dir = [jnp.stack([c_[n] for c_ in csts]) for n in ("tri", "trit", "incl", "strict")]
    shared = [csts[0][n] for n in ("eye", "m16", "q32", "q64", "bd")]
    rbs = [functools.partial(_scan_block_index, nblk=nblk, reverse=rev) for rev in (False, True)]
    col = lambda rb, j: pl.BlockSpec((SCAN_ROWS, DELTA_W), lambda i: (rb(i), j))
    stream = lambda rb: [col(rb, 0), col(rb, 1), col(rb, 2), pl.BlockSpec((SCAN_ROWS, SM_W), lambda i: (rb(i), 0)),
                         pl.BlockSpec((HEAD, SCAN_ROWS), lambda i: (0, rb(i)))]
    consts = [prow, pcol] + per_dir + shared
    out = jax.ShapeDtypeStruct((ntok, DELTA_W), BF16)
    return pl.pallas_call(
        _delta_kernel,
        out_shape=(out, out),
        grid=(nblk,),
        in_specs=stream(rbs[0]) + stream(rbs[1]) + [_full_spec(a) for a in consts],
        out_specs=tuple(pl.BlockSpec((SCAN_ROWS, DELTA_W), lambda i, rb=rb: (rb(i), 0)) for rb in rbs),
        scratch_shapes=[pltpu.VMEM((2 * DELTA_HEADS, HEAD, HEAD), F32),
                        pltpu.VMEM((2, SCAN_ROWS, HEAD), F32),
                        pltpu.VMEM((2, SCAN_ROWS, HEAD), F32),
                        pltpu.VMEM((2, SCAN_ROWS // CHUNK, 2 * DELTA_HEADS, CHUNK), F32)],
        compiler_params=_cparams(("arbitrary",)),
        name="delta",
    )(qkv, qkv, qkv, zs, zt, qkv, qkv, qkv, zs, zt, *consts)


def _head_rms(x, w_ref, n_heads):
    parts = []
    for h in range(n_heads):
        xh = x[:, h * HEAD:(h + 1) * HEAD]
        parts.append(xh * lax.rsqrt(jnp.mean(xh * xh, axis=-1, keepdims=True) + RMS_EPS) * w_ref[...])
    return jnp.concatenate(parts, axis=1)


def _up_kernel(oaf_ref, oab_ref, od_ref, oef_ref, oeb_ref, gg_ref, eg_ref, ga_ref, gd_ref, ge_ref,
               nwa_ref, nwd_ref, nwe_ref, wa_ref, wd_ref, we_ref, y_ref, a_scr, d_scr, e_scr, *, lam_init):
    @pl.when(pl.program_id(1) == 0)
    def _():
        gg = gg_ref[...].astype(F32)
        eg = eg_ref[...].astype(F32)
        oa = oaf_ref[...].astype(F32) + oab_ref[...].astype(F32)
        oe = oef_ref[...].astype(F32) + oeb_ref[...].astype(F32)
        a_scr[...] = _bf(_head_rms(oa, nwa_ref, GLA_HEADS) * _silu(gg))
        d_scr[...] = _bf(_head_rms(od_ref[...].astype(F32), nwd_ref, DIFF_HEADS) * (1.0 - lam_init))
        e_scr[...] = _bf(_head_rms(oe, nwe_ref, DELTA_HEADS) * _silu(eg))

    y = _sigmoid(ga_ref[...].astype(F32)) * _dot(a_scr[...], wa_ref[...])
    y = y + _sigmoid(gd_ref[...].astype(F32)) * _dot(d_scr[...], wd_ref[...])
    y = y + _sigmoid(ge_ref[...].astype(F32)) * _dot(e_scr[...], we_ref[...])
    y_ref[...] = _bf(y)


def _up(oaf, oab, od, oef, oeb, z, nwa, nwd, nwe, wa, wd, we, layer, lam_init, tm):
    ntok = z.shape[0]
    tn = UP_TN
    rows = lambda w: pl.BlockSpec((tm, w), lambda i, j: (i, 0))
    zcol = lambda col, w: pl.BlockSpec((tm, w), lambda i, j: (i, col // w))
    gate = lambda b: pl.BlockSpec((tm, tn), lambda i, j: (i, (COL_MG + b * D_MODEL) // tn + j))
    vec = pl.BlockSpec((1, HEAD), lambda i, j: (0, 0))
    wspec = lambda k: pl.BlockSpec((None, None, k, tn), lambda i, j: (layer, j, 0, 0))
    return pl.pallas_call(
        functools.partial(_up_kernel, lam_init=lam_init),
        out_shape=jax.ShapeDtypeStruct((ntok, D_MODEL), BF16),
        grid=(ntok // tm, D_MODEL // tn),
        in_specs=[rows(GLA_W), rows(GLA_W), rows(DIFF_W), rows(DELTA_W), rows(DELTA_W),
                  zcol(COL_GG, GLA_W), zcol(COL_EG, DELTA_W), gate(0), gate(1), gate(2),
                  vec, vec, vec, wspec(GLA_W), wspec(DIFF_W), wspec(DELTA_W)],
        out_specs=pl.BlockSpec((tm, tn), lambda i, j: (i, j)),
        scratch_shapes=[pltpu.VMEM((tm, GLA_W), BF16), pltpu.VMEM((tm, DIFF_W), BF16), pltpu.VMEM((tm, DELTA_W), BF16)],
        compiler_params=_cparams(("arbitrary", "arbitrary")),
        name="up_merge",
    )(oaf, oab, od, oef, oeb, z, z, z, z, z, nwa.reshape(1, HEAD), nwd.reshape(1, HEAD), nwe.reshape(1, HEAD),
      wa, wd, we)


def _oproj_kernel(y_ref, x_ref, wo_ref, nw_ref, gate_ref, o_ref, t_scr):
    t_scr[...] = _dot(y_ref[...], wo_ref[...])
    _gated_residual(x_ref, t_scr, nw_ref, gate_ref, o_ref, pl.program_id(0) * x_ref.shape[0])


def _oproj(y, xc, wo, norm_w, mods, layer, tm):
    ntok, d = xc.shape
    return pl.pallas_call(
        _oproj_kernel,
        out_shape=jax.ShapeDtypeStruct((ntok, d), F32),
        grid=(ntok // tm,),
        in_specs=[
            pl.BlockSpec((tm, d), lambda i: (i, 0)),
            pl.BlockSpec((tm, d), lambda i: (i, 0)),
            pl.BlockSpec((None, d, d), lambda i: (layer, 0, 0), pipeline_mode=pl.Buffered(1)),
            pl.BlockSpec((1, d), lambda i: (0, 0)),
            pl.BlockSpec((8, d), lambda i: (0, 2)),
        ],
        out_specs=pl.BlockSpec((tm, d), lambda i: (i, 0)),
        scratch_shapes=[pltpu.VMEM((tm, d), F32)],
        compiler_params=_cparams(("arbitrary",)),
        name="out_proj",
    )(y, xc, wo, norm_w.reshape(1, d), mods)


def _ffn_kernel(x_ref, nw_ref, sh_ref, sc_ref, gate_ref, pw_ref, w1_ref, w3_ref, w2_ref, o_ref, h_scr, acc_scr):
    j = pl.program_id(1)
    row0 = pl.program_id(0) * x_ref.shape[0]

    @pl.when(j == 0)
    def _():
        _norm_modulate(x_ref, nw_ref, sh_ref, sc_ref, h_scr, row0)
        acc_scr[...] = jnp.zeros_like(acc_scr)

    h = h_scr[...]
    u = _silu(_dot(h, w1_ref[...])) * _dot(h, w3_ref[...])
    acc_scr[...] += _dot(_bf(u), w2_ref[...])

    @pl.when(j == pl.num_programs(1) - 1)
    def _():
        _gated_residual(x_ref, acc_scr, pw_ref, gate_ref, o_ref, row0)


def _ffn(xc, pre_w, post_w, mods, w1, w3, w2, layer, tm):
    ntok, d = xc.shape
    dff = w2.shape[1]
    return pl.pallas_call(
        _ffn_kernel,
        out_shape=jax.ShapeDtypeStruct((ntok, d), F32),
        grid=(ntok // tm, dff // FF_TN),
        in_specs=[
            pl.BlockSpec((tm, d), lambda i, j: (i, 0)),
            pl.BlockSpec((1, d), lambda i, j: (0, 0)),
            pl.BlockSpec((8, d), lambda i, j: (0, 3)),
            pl.BlockSpec((8, d), lambda i, j: (0, 4)),
            pl.BlockSpec((8, d), lambda i, j: (0, 5)),
            pl.BlockSpec((1, d), lambda i, j: (0, 0)),
            pl.BlockSpec((None, None, d, FF_TN), lambda i, j: (layer, j, 0, 0)),
            pl.BlockSpec((None, None, d, FF_TN), lambda i, j: (layer, j, 0, 0)),
            pl.BlockSpec((None, FF_TN, d), lambda i, j: (layer, j, 0)),
        ],
        out_specs=pl.BlockSpec((tm, d), lambda i, j: (i, 0), pipeline_mode=pl.Buffered(1)),
        scratch_shapes=[pltpu.VMEM((tm, d), BF16), pltpu.VMEM((tm, d), F32)],
        compiler_params=_cparams(("arbitrary", "arbitrary")),
        name="ffn",
    )(xc, pre_w.reshape(1, d), mods, mods, mods, post_w.reshape(1, d), w1, w3, w2)


W_IN_MAIN = ((0, 1536), (1568, 2080), (6720, 7744), (3616, 6688), (2080, 3616), (7744, 13888))
W_IN_SMALL = ((1536, 1568), (6688, 6720))
PREP_ROWS = 256


def _prep_w_in_kernel(w_ref, o_ref, s_ref):
    off = 0
    for a, b in W_IN_MAIN:
        o_ref[:, off:off + b - a] = _bf(w_ref[:, a:b])
        off += b - a
    off = 0
    for a, b in W_IN_SMALL:
        s_ref[:, off:off + b - a] = _bf(w_ref[:, a:b])
        off += b - a
    s_ref[:, off:] = jnp.zeros((s_ref.shape[0], s_ref.shape[1] - off), BF16)


def _prep_w_in(w):
    depth, d, n_in = w.shape
    spec = lambda n: pl.BlockSpec((None, PREP_ROWS, n), lambda l, i: (l, i, 0))
    return pl.pallas_call(
        _prep_w_in_kernel,
        out_shape=(jax.ShapeDtypeStruct((depth, d, NZ), BF16), jax.ShapeDtypeStruct((depth, d, SM_W), BF16)),
        grid=(depth, d // PREP_ROWS),
        in_specs=[spec(n_in)],
        out_specs=(spec(NZ), spec(SM_W)),
        compiler_params=_cparams(("arbitrary", "arbitrary")),
        name="prep_w_in",
    )(w)


def _cast_kernel(w_ref, o_ref):
    tn = o_ref.shape[2]
    for t in range(o_ref.shape[0]):
        o_ref[t] = _bf(w_ref[:, t * tn:(t + 1) * tn])


def _cast_bf16(w, tn=None):
    depth, rows, n = w.shape
    tn = n if tn is None else tn
    out = pl.pallas_call(
        _cast_kernel,
        out_shape=jax.ShapeDtypeStruct((depth, n // tn, rows, tn), BF16),
        grid=(depth, rows // PREP_ROWS),
        in_specs=[pl.BlockSpec((None, PREP_ROWS, n), lambda l, i: (l, i, 0))],
        out_specs=pl.BlockSpec((None, n // tn, PREP_ROWS, tn), lambda l, i: (l, 0, i, 0)),
        compiler_params=_cparams(("arbitrary", "arbitrary")),
        name="cast_bf16",
    )(w)
    return out.reshape(depth, rows, n) if tn == n else out


def _rope_tables(seq):
    rows = seq // GRID_W
    row_ids = jnp.repeat(jnp.arange(rows, dtype=F32), GRID_W)
    col_ids = jnp.tile(jnp.arange(GRID_W, dtype=F32), rows)
    half = DIFF_DQK // 2
    inv = 1.0 / (ROPE_BASE ** (jnp.arange(0, half, 2, dtype=F32) / half))
    ang_r = row_ids[:, None] * inv
    ang_c = col_ids[:, None] * inv
    cr, sr, cc, sc = jnp.cos(ang_r), jnp.sin(ang_r), jnp.cos(ang_c), jnp.sin(ang_c)
    cos_lat = jnp.tile(jnp.concatenate([cr, cr, cc, cc], axis=1), (1, 2))
    sin_lat = jnp.tile(jnp.concatenate([-sr, sr, -sc, sc], axis=1), (1, 2))
    cos_t = jnp.concatenate([jnp.ones((CTX_LEN, HEAD), F32), cos_lat], axis=0)
    sin_t = jnp.concatenate([jnp.zeros((CTX_LEN, HEAD), F32), sin_lat], axis=0)
    return cos_t, sin_t


def _lane_params(a_log, dt_bias):
    flat_a = a_log.reshape(-1)
    flat_b = dt_bias.reshape(-1)
    n = flat_a.shape[0]
    pa = jnp.zeros((HEAD,), F32).at[SM_A:SM_A + n].set(flat_a)
    pb = jnp.zeros((HEAD,), F32).at[SM_A:SM_A + n].set(flat_b)
    prow = jnp.zeros((8, HEAD), F32).at[0].set(pa).at[1].set(pb)
    pcol = jnp.zeros((HEAD, HEAD), F32).at[:, 0].set(pa).at[:, 1].set(pb)
    return prow, pcol


def _gla_gate_weights(w2, bias):
    wpad = jnp.zeros((2, HEAD, GLA_W), F32)
    for dirn in range(2):
        wpad = wpad.at[dirn, GLA_GATE_RANK * dirn:GLA_GATE_RANK * (dirn + 1)].set(w2[dirn])
    return _bf(wpad), bias.reshape(2, 1, GLA_W)


def kernel(x, c, ctx, c_ctx, ada_w, ada_b, mix_pre_w, mix_post_w, ffn_pre_w, ffn_post_w, w_in, gla_gate_w2, gla_gate_b, gla_norm_w, diff_lambda, diff_norm_w, delta_conv_w, delta_a_log, delta_dt_bias, delta_norm_w, w_up_gla, w_up_diff, w_up_delta, w_o, ffn_w1, ffn_w3, ffn_w2):
    seq = x.shape[1]
    ntok = CTX_LEN + seq
    tm = ROW_TILE if ntok % ROW_TILE == 0 else SCAN_ROWS
    in_tm = IN_TM if ntok % IN_TM == 0 else SCAN_ROWS
    xc = jnp.concatenate([ctx[0], x[0]], axis=0)
    cond = jnp.concatenate([c, c_ctx[None, :], jnp.zeros((COND_ROWS - 2, D_MODEL), F32)], axis=0)
    mods_all = _ada(_bf(jax.nn.silu(cond)), ada_w, ada_b)
    cos_t, sin_t = _rope_tables(seq)
    kv_tile = ROW_TILE if ntok % ROW_TILE == 0 else SCAN_ROWS
    w_main, w_small = _prep_w_in(w_in)
    wa, wd, we = _cast_bf16(w_up_gla, UP_TN), _cast_bf16(w_up_diff, UP_TN), _cast_bf16(w_up_delta, UP_TN)
    wo = _cast_bf16(w_o)
    w1, w3, w2 = _cast_bf16(ffn_w1, FF_TN), _cast_bf16(ffn_w3, FF_TN), _cast_bf16(ffn_w2)

    for layer in range(DEPTH):
        lam_init = 0.8 - 0.6 * math.exp(-0.3 * layer)
        mods = mods_all[layer]
        z, zs = _inproj(xc, mix_pre_w[layer], mods, w_main, w_small, layer, in_tm)

        oa = _gla(z, zs, *_gla_gate_weights(gla_gate_w2[layer], gla_gate_b[layer]))

        qr, kr, vr = _rope(z, cos_t, sin_t)
        od = _attn(diff_lambda[layer], qr, kr, vr, kv_tile, lam_init)

        qkv = _conv(z, delta_conv_w[layer])
        zt = zs[:, 0:HEAD].T
        prow, pcol = _lane_params(delta_a_log[layer], delta_dt_bias[layer])
        oe = _delta(qkv, zs, zt, prow, pcol)

        y = _up(oa[0], oa[1], od, oe[0], oe[1], z, gla_norm_w[layer], diff_norm_w[layer], delta_norm_w[layer],
                wa, wd, we, layer, lam_init, tm)
        xc = _oproj(y, xc, wo, mix_post_w[layer], mods, layer, tm)
        xc = _ffn(xc, ffn_pre_w[layer], ffn_post_w[layer], mods, w1, w3, w2, layer, tm)
    return xc[CTX_LEN:][None]
```
